```python
import jax
import jax.numpy as jnp
from jax import lax
import numpy as np


D_MODEL = 1024
BATCH = 16
SEQ = 2048
DEPTH = 4

HEAD_DIM = 64
ATTN_Q_HEADS = 8
ATTN_KV_HEADS = 2
ATTN_GROUP = ATTN_Q_HEADS // ATTN_KV_HEADS
ATTN_WIDTH = ATTN_Q_HEADS * HEAD_DIM
ATTN_KV_WIDTH = ATTN_KV_HEADS * HEAD_DIM
WINDOW = 128
ATTN_BLOCK = 128
RWKV_HEADS = 8
RWKV_WIDTH = RWKV_HEADS * HEAD_DIM
DECAY_LORA = 64
ICLR_LORA = 64
GATE_LORA = 128
RWKV_SIZES = (RWKV_WIDTH, RWKV_WIDTH, RWKV_WIDTH, DECAY_LORA, ICLR_LORA, GATE_LORA)
RWKV_COLS = sum(RWKV_SIZES)
RWKV_LN_EPS = 64e-5
RET_HEADS = 8
RET_WIDTH = RET_HEADS * HEAD_DIM
RET_CHUNK = 128
ROPE_BASE = 10000.0
N_BRANCH = 3
IN_SIZES = (ATTN_WIDTH, ATTN_KV_WIDTH, ATTN_KV_WIDTH, RWKV_COLS,
            RET_WIDTH, RET_WIDTH, RET_WIDTH, RET_WIDTH, N_BRANCH * D_MODEL)
IN_WIDTH = sum(IN_SIZES)
D_FF = -((-8 * D_MODEL) // (3 * 256)) * 256
NORM_EPS = 1e-6

kernel_name = 'hybrid_swa_rwkv7_retention_block'


def _split(t, sizes):
    out, start = [], 0
    for s in sizes:
        out.append(t[..., start:start + s])
        start += s
    return out


def _rms(x, eps=NORM_EPS):
    xf = x.astype(jnp.float32)
    return (xf * lax.rsqrt(jnp.mean(xf * xf, axis=-1, keepdims=True) + eps)).astype(x.dtype)


def sliding_window_attention(q, k, v, sinks):
    B, T = q.shape[0], q.shape[1]
    C = ATTN_BLOCK
    NB = T // C
    qb = q.reshape(B, NB, C, ATTN_KV_HEADS, ATTN_GROUP, HEAD_DIM)

    def band(t):
        tb = t.reshape(B, NB, C, ATTN_KV_HEADS, HEAD_DIM)
        prev = jnp.concatenate([jnp.zeros_like(tb[:, :1]), tb[:, :-1]], axis=1)
        return jnp.concatenate([prev, tb], axis=2)

    kb, vb = band(k), band(v)
    s = jnp.einsum('bnqhgd,bnshd->bnhgqs', qb, kb).astype(jnp.float32) * (HEAD_DIM ** -0.5)
    qi = jnp.arange(C)[:, None]
    si = jnp.arange(2 * C)[None, :]
    rel = qi + C - si
    key_pos = jnp.arange(NB)[:, None, None] * C + si[None] - C
    mask = (rel >= 0) & (rel < WINDOW) & (key_pos >= 0)
    s = jnp.where(mask[None, :, None, None], s, -jnp.inf)
    sink = sinks.astype(jnp.float32).reshape(ATTN_KV_HEADS, ATTN_GROUP)[None, None, :, :, None, None]
    m = jnp.maximum(jnp.max(s, axis=-1, keepdims=True), sink)
    p = jnp.exp(s - m)
    p = p / (jnp.sum(p, axis=-1, keepdims=True) + jnp.exp(sink - m))
    o = jnp.einsum('bnhgqs,bnshd->bnqhgd', p.astype(v.dtype), vb)
    return o.reshape(B, T, ATTN_WIDTH)


def rwkv7_time_mix(h, shift_mu, w0, w2, a0, a2, g2, k_k, k_a, r_k, lnx_g, lnx_b):
    B, T = h.shape[0], h.shape[1]
    H, N = RWKV_HEADS, HEAD_DIM
    h_prev = jnp.pad(h, ((0, 0), (1, 0), (0, 0)))[:, :-1]
    z = h + shift_mu * (h_prev - h)
    r, k, v, wd, ad, gd = _split(z, RWKV_SIZES)
    w_log = -jax.nn.softplus(-(w0 + jnp.tanh(wd) @ w2)) - 0.5
    decay = jnp.exp(-jnp.exp(w_log.astype(jnp.float32)))
    a = jax.nn.sigmoid(a0 + ad @ a2)
    g = jax.nn.sigmoid(gd) @ g2
    kk = (k * k_k).reshape(B, T, H, N).astype(jnp.float32)
    kk = kk / jnp.maximum(jnp.linalg.norm(kk, axis=-1, keepdims=True), 1e-12)
    k = k * (1.0 + (a - 1.0) * k_a)

    def heads(t):
        return t.reshape(B, T, H, N).astype(jnp.float32)

    r_h, k_h, v_h, a_h, w_h = heads(r), heads(k), heads(v), heads(a), heads(decay)
    xs = tuple(t.transpose(1, 0, 2, 3) for t in (r_h, w_h, k_h, v_h, -kk, kk * a_h))

    def step(S, inp):
        r_t, w_t, k_t, v_t, a_t, b_t = inp
        Sa = jnp.einsum('bhvk,bhk->bhv', S, a_t)
        S = S * w_t[:, :, None, :] + Sa[..., None] * b_t[:, :, None, :] + v_t[..., None] * k_t[:, :, None, :]
        return S, jnp.einsum('bhvk,bhk->bhv', S, r_t)

    S0 = jnp.zeros((B, H, N, N), jnp.float32)
    _, y = lax.scan(step, S0, xs)
    y = y.transpose(1, 0, 2, 3)
    mu = jnp.mean(y, axis=-1, keepdims=True)
    var = jnp.mean(jnp.square(y - mu), axis=-1, keepdims=True)
    y = ((y - mu) * lax.rsqrt(var + RWKV_LN_EPS)).reshape(B, T, RWKV_WIDTH)
    y = y * lnx_g.astype(jnp.float32) + lnx_b.astype(jnp.float32)
    bonus = jnp.sum(r_h * k_h * r_k.astype(jnp.float32), axis=-1, keepdims=True) * v_h
    y = (y + bonus.reshape(B, T, RWKV_WIDTH)) * g.astype(jnp.float32)
    return y.astype(h.dtype)


def _rotary(x, pos):
    half = HEAD_DIM // 2
    inv_freq = 1.0 / (ROPE_BASE ** (jnp.arange(half, dtype=jnp.float32) * 2.0 / HEAD_DIM))
    ang = pos.astype(jnp.float32)[:, None] * inv_freq[None, :]
    cos, sin = jnp.cos(ang)[:, None, :], jnp.sin(ang)[:, None, :]
    x1, x2 = x[..., :half], x[..., half:]
    return jnp.concatenate([x1 * cos - x2 * sin, x2 * cos + x1 * sin], axis=-1)


def retention(q, k, v, gate):
    B, T = q.shape[0], q.shape[1]
    H, d, C = RET_HEADS, HEAD_DIM, RET_CHUNK
    NC = T // C
    pos = jnp.arange(T)
    qh = _rotary(q.reshape(B, T, H, d).astype(jnp.float32), pos)
    kh = _rotary(k.reshape(B, T, H, d).astype(jnp.float32), pos) * (d ** -0.5)
    vh = v.reshape(B, T, H, d).astype(jnp.float32)
    log_gamma = jnp.log1p(-jnp.power(2.0, -5.0 - jnp.arange(H, dtype=jnp.float32)))
    idx = jnp.arange(C, dtype=jnp.float32)
    diff = idx[:, None] - idx[None, :]
    dmat = jnp.where(diff >= 0, jnp.exp(log_gamma[:, None, None] * jnp.maximum(diff, 0.0)), 0.0)
    xi = jnp.exp(log_gamma[:, None] * (idx[None, :] + 1.0))
    zeta = jnp.exp(log_gamma[:, None] * (C - 1.0 - idx[None, :]))
    chunk_decay = jnp.exp(log_gamma * C)
    qc = qh.reshape(B, NC, C, H, d)
    kc = kh.reshape(B, NC, C, H, d)
    vc = vh.reshape(B, NC, C, H, d)
    s = jnp.einsum('bnihd,bnjhd->bnhij', qc, kc) * dmat
    inner = jnp.einsum('bnhij,bnjhe->bnihe', s, vc)
    kv = jnp.einsum('bnjhd,bnjhe,hj->nbhde', kc, vc, zeta)

    def step(R, kv_n):
        return R * chunk_decay[None, :, None, None] + kv_n, R

    _, r_prev = lax.scan(step, jnp.zeros((B, H, d, d), jnp.float32), kv)
    cross = jnp.einsum('bnihd,nbhde,hi->bnihe', qc, r_prev, xi)
    o = _rms((inner + cross).reshape(B, T, H, d)).reshape(B, T, RET_WIDTH)
    o = o * jax.nn.silu(gate.astype(jnp.float32))
    return o.astype(q.dtype)


def hybrid_layer(x, norm1_g, w_in, attn_q_norm_g, attn_k_norm_g, attn_sinks, w_attn_o,
                 rwkv_shift_mu, rwkv_w0, rwkv_w2, rwkv_a0, rwkv_a2, rwkv_g2, rwkv_k_k, rwkv_k_a,
                 rwkv_r_k, rwkv_lnx_g, rwkv_lnx_b, w_rwkv_o, w_ret_o, w_out,
                 norm2_g, w_ffn_gate, w_ffn_up, w_ffn_down):
    B, T, D = x.shape
    h = _rms(x) * norm1_g
    proj = h @ w_in
    aq, ak, av, rw, rq, rk, rv, rg, gates = _split(proj, IN_SIZES)
    aq = _rms(aq.reshape(B, T, ATTN_Q_HEADS, HEAD_DIM)) * attn_q_norm_g
    ak = _rms(ak.reshape(B, T, ATTN_KV_HEADS, HEAD_DIM)) * attn_k_norm_g
    av = av.reshape(B, T, ATTN_KV_HEADS, HEAD_DIM)
    o_a = sliding_window_attention(aq, ak, av, attn_sinks) @ w_attn_o
    o_b = rwkv7_time_mix(rw, rwkv_shift_mu, rwkv_w0, rwkv_w2, rwkv_a0, rwkv_a2, rwkv_g2,
                         rwkv_k_k, rwkv_k_a, rwkv_r_k, rwkv_lnx_g, rwkv_lnx_b) @ w_rwkv_o
    o_c = retention(rq, rk, rv, rg) @ w_ret_o
    g = jax.nn.sigmoid(gates).reshape(B, T, N_BRANCH, D)
    mixed = g[:, :, 0] * o_a + g[:, :, 1] * o_b + g[:, :, 2] * o_c
    x = x + mixed @ w_out
    h2 = _rms(x) * norm2_g
    x = x + (jax.nn.silu(h2 @ w_ffn_gate) * (h2 @ w_ffn_up)) @ w_ffn_down
    return x


def setup_inputs(seed: int = 0) -> dict:
    key = jax.random.key(seed)
    ks = jax.random.split(key, 25)
    L = DEPTH

    def nrm(k, shape, scale):
        return jax.random.normal(k, shape, jnp.float32) * scale

    return {
        'x': nrm(ks[0], (BATCH, SEQ, D_MODEL), 1.0),
        'norm1_g': 1.0 + nrm(ks[1], (L, D_MODEL), 0.02),
        'w_in': nrm(ks[2], (L, D_MODEL, IN_WIDTH), D_MODEL ** -0.5),
        'attn_q_norm_g': 1.0 + nrm(ks[3], (L, HEAD_DIM), 0.02),
        'attn_k_norm_g': 1.0 + nrm(ks[4], (L, HEAD_DIM), 0.02),
        'attn_sinks': nrm(ks[5], (L, ATTN_Q_HEADS), 0.5),
        'w_attn_o': nrm(ks[6], (L, ATTN_WIDTH, D_MODEL), ATTN_WIDTH ** -0.5),
        'rwkv_shift_mu': jax.random.uniform(ks[7], (L, RWKV_COLS), jnp.float32),
        'rwkv_w0': jax.random.uniform(ks[8], (L, RWKV_WIDTH), jnp.float32, minval=-6.0, maxval=1.0),
        'rwkv_w2': nrm(ks[9], (L, DECAY_LORA, RWKV_WIDTH), 0.1 * DECAY_LORA ** -0.5),
        'rwkv_a0': nrm(ks[10], (L, RWKV_WIDTH), 0.1),
        'rwkv_a2': nrm(ks[11], (L, ICLR_LORA, RWKV_WIDTH), ICLR_LORA ** -0.5),
        'rwkv_g2': nrm(ks[12], (L, GATE_LORA, RWKV_WIDTH), GATE_LORA ** -0.5),
        'rwkv_k_k': 0.85 + nrm(ks[13], (L, RWKV_WIDTH), 0.02),
        'rwkv_k_a': 1.0 + nrm(ks[14], (L, RWKV_WIDTH), 0.02),
        'rwkv_r_k': nrm(ks[15], (L, RWKV_HEADS, HEAD_DIM), 0.1),
        'rwkv_lnx_g': 1.0 + nrm(ks[16], (L, RWKV_WIDTH), 0.02),
        'rwkv_lnx_b': nrm(ks[17], (L, RWKV_WIDTH), 0.02),
        'w_rwkv_o': nrm(ks[18], (L, RWKV_WIDTH, D_MODEL), RWKV_WIDTH ** -0.5),
        'w_ret_o': nrm(ks[19], (L, RET_WIDTH, D_MODEL), RET_WIDTH ** -0.5),
        'w_out': nrm(ks[20], (L, D_MODEL, D_MODEL), D_MODEL ** -0.5),
        'norm2_g': 1.0 + nrm(ks[21], (L, D_MODEL), 0.02),
        'w_ffn_gate': nrm(ks[22], (L, D_MODEL, D_FF), D_MODEL ** -0.5),
        'w_ffn_up': nrm(ks[23], (L, D_MODEL, D_FF), D_MODEL ** -0.5),
        'w_ffn_down': nrm(ks[24], (L, D_FF, D_MODEL), D_FF ** -0.5),
    }


def reference(x, norm1_g, w_in, attn_q_norm_g, attn_k_norm_g, attn_sinks, w_attn_o,
              rwkv_shift_mu, rwkv_w0, rwkv_w2, rwkv_a0, rwkv_a2, rwkv_g2, rwkv_k_k, rwkv_k_a,
              rwkv_r_k, rwkv_lnx_g, rwkv_lnx_b, w_rwkv_o, w_ret_o, w_out,
              norm2_g, w_ffn_gate, w_ffn_up, w_ffn_down):
    for l in range(DEPTH):
        x = hybrid_layer(x, norm1_g[l], w_in[l], attn_q_norm_g[l], attn_k_norm_g[l], attn_sinks[l],
                         w_attn_o[l], rwkv_shift_mu[l], rwkv_w0[l], rwkv_w2[l], rwkv_a0[l], rwkv_a2[l],
                         rwkv_g2[l], rwkv_k_k[l], rwkv_k_a[l], rwkv_r_k[l], rwkv_lnx_g[l], rwkv_lnx_b[l],
                         w_rwkv_o[l], w_ret_o[l], w_out[l], norm2_g[l], w_ffn_gate[l], w_ffn_up[l],
                         w_ffn_down[l])
    return x
```

```python
import functools

import jax
import jax.numpy as jnp
from jax import lax
from jax.experimental import pallas as pl
from jax.experimental.pallas import tpu as pltpu

F32 = jnp.float32
BF16 = jnp.bfloat16

HEAD = 64
LANES = 128
D_MODEL = 1024
N_HEADS = 8
WIDTH = N_HEADS * HEAD
N_PAIRS = WIDTH // LANES
ATTN_BLOCK = 128
RET_CHUNK = 128
RWKV_CHUNK = 64
NORM_EPS = 1e-6
RWKV_LN_EPS = 64e-5
ROPE_BASE = 10000.0
VMEM_LIMIT = 56 * 1024 * 1024

COL_RET = 0
COL_GATES = 2048
COL_RWKV_R = 5120
COL_RWKV_K = 5632
COL_RWKV_V = 6144
COL_ATTN_Q = 6656
COL_RWKV_LORA = 7168
COL_ATTN_K = 7424
COL_ATTN_V = 7552
IN_WIDTH = 7680

NN = (((1,), (0,)), ((), ()))
NT = (((1,), (1,)), ((), ()))
TN = (((0,), (0,)), ((), ()))
HI = lax.Precision.HIGHEST


def _dot(a, b, dims=NN, precision=None):
    return lax.dot_general(a, b, dims, precision=precision, preferred_element_type=F32)


def _seg_sum(x, bd):
    hi = x.astype(BF16)
    lo = (x - hi.astype(F32)).astype(BF16)
    return _dot(hi, bd) + _dot(lo, bd)


def _params(*sem):
    return pltpu.CompilerParams(dimension_semantics=sem, vmem_limit_bytes=VMEM_LIMIT)


def _resident(shape):
    nd = len(shape)
    return pl.BlockSpec(shape, lambda *_: (0,) * nd, pipeline_mode=pl.Buffered(1))


def _rms_matmul_kernel(x_ref, g_ref, w_ref, o_ref, h_ref):
    @pl.when(pl.program_id(1) == 0)
    def _():
        x = x_ref[...]
        ms = jnp.mean(x * x, axis=-1, keepdims=True)
        h_ref[...] = (x * lax.rsqrt(ms + NORM_EPS) * g_ref[...]).astype(BF16)

    o_ref[...] = _dot(h_ref[...], w_ref[...]).astype(o_ref.dtype)


def _rms_matmul(x, g, w, tm, tn):
    n, d = x.shape
    cols = w.shape[1]
    return pl.pallas_call(
        _rms_matmul_kernel,
        grid=(n // tm, cols // tn),
        in_specs=[
            pl.BlockSpec((tm, d), lambda i, j: (i, 0)),
            pl.BlockSpec((1, d), lambda i, j: (0, 0)),
            pl.BlockSpec((d, tn), lambda i, j: (0, j)),
        ],
        out_specs=pl.BlockSpec((tm, tn), lambda i, j: (i, j)),
        out_shape=jax.ShapeDtypeStruct((n, cols), BF16),
        scratch_shapes=[pltpu.VMEM((tm, d), BF16)],
        compiler_params=_params("arbitrary", "arbitrary"),
        name="in_proj",
    )(x, g, w)


def _attn_kernel(sink_ref, q_ref, kc_ref, vc_ref, kp_ref, vp_ref, gq_ref, gk_ref, bd_ref, o_ref, *, tq):
    i = pl.program_id(1)
    bd = bd_ref[...]
    q = q_ref[0].astype(F32)
    qn = q * lax.rsqrt(_seg_sum(q * q, bd) * (1.0 / HEAD) + NORM_EPS) * (gq_ref[...] * HEAD ** -0.5)
    k = jnp.concatenate([kp_ref[0], kc_ref[0]], axis=0).astype(F32)
    kn = k * lax.rsqrt(_seg_sum(k * k, bd[:LANES, :LANES]) * (1.0 / HEAD) + NORM_EPS) * gk_ref[...]
    v = jnp.concatenate([vp_ref[0], vc_ref[0]], axis=0).astype(F32)

    lane_k = lax.broadcasted_iota(jnp.int32, k.shape, 1)
    lane_q = lax.broadcasted_iota(jnp.int32, (ATTN_BLOCK, LANES), 1)
    low_q = lane_q < HEAD
    row = lax.broadcasted_iota(jnp.int32, (2 * ATTN_BLOCK, 2 * ATTN_BLOCK), 0)
    col = lax.broadcasted_iota(jnp.int32, (2 * ATTN_BLOCK, 2 * ATTN_BLOCK), 1)
    rel = (row & (ATTN_BLOCK - 1)) + ATTN_BLOCK - col
    band = (rel >= 0) & (rel < ATTN_BLOCK)
    row1 = lax.broadcasted_iota(jnp.int32, (2 * ATTN_BLOCK, 1), 0)

    for e in range(2):
        in_e = (lane_k >= e * HEAD) & (lane_k < (e + 1) * HEAD)
        k_e = jnp.where(in_e, kn, 0.0)
        k_dup = (k_e + pltpu.roll(k_e, HEAD, 1)).astype(BF16)
        v_e = jnp.where(in_e, v, 0.0)
        v_lo = v_e if e == 0 else pltpu.roll(v_e, HEAD, 1)
        v_hi = pltpu.roll(v_lo, HEAD, 1)
        v_lo = v_lo.astype(BF16)
        v_hi = v_hi.astype(BF16)
        for n in range(tq // ATTN_BLOCK):
            r0 = n * ATTN_BLOCK
            first = (i * (tq // ATTN_BLOCK) + n) == 0
            valid = band & (col >= jnp.where(first, ATTN_BLOCK, 0))
            k_blk = k_dup[r0:r0 + 2 * ATTN_BLOCK]
            for c in (2 * e, 2 * e + 1):
                qc = qn[r0:r0 + ATTN_BLOCK, c * LANES:(c + 1) * LANES]
                qs = jnp.concatenate([jnp.where(low_q, qc, 0.0), jnp.where(low_q, 0.0, qc)], axis=0)
                s = _dot(qs.astype(BF16), k_blk, NT)
                s = jnp.where(valid, s, -jnp.inf)
                sink = jnp.where(row1 < ATTN_BLOCK, sink_ref[2 * c], sink_ref[2 * c + 1])
                m = jnp.maximum(jnp.max(s, axis=-1, keepdims=True), sink)
                p = jnp.exp(s - m)
                den = jnp.sum(p, axis=-1, keepdims=True) + jnp.exp(sink - m)
                pb = p.astype(BF16)
                o = (_dot(pb[:ATTN_BLOCK], v_lo[r0:r0 + 2 * ATTN_BLOCK])
                     + _dot(pb[ATTN_BLOCK:], v_hi[r0:r0 + 2 * ATTN_BLOCK]))
                inv = 1.0 / den
                o = o * jnp.where(low_q, inv[:ATTN_BLOCK], inv[ATTN_BLOCK:])
                o_ref[0, r0:r0 + ATTN_BLOCK, c * LANES:(c + 1) * LANES] = o.astype(o_ref.dtype)


def _attention(proj, sinks, gq, gk, bd, tq):
    b, t, _ = proj.shape
    nb = tq // ATTN_BLOCK
    kernel = functools.partial(_attn_kernel, tq=tq)
    return pl.pallas_call(
        kernel,
        grid=(b, t // tq),
        in_specs=[
            pl.BlockSpec(memory_space=pltpu.SMEM),
            pl.BlockSpec((1, tq, WIDTH), lambda bi, i: (bi, i, COL_ATTN_Q // WIDTH)),
            pl.BlockSpec((1, tq, LANES), lambda bi, i: (bi, i, COL_ATTN_K // LANES)),
            pl.BlockSpec((1, tq, LANES), lambda bi, i: (bi, i, COL_ATTN_V // LANES)),
            pl.BlockSpec((1, ATTN_BLOCK, LANES),
                         lambda bi, i: (bi, jnp.maximum(i * nb - 1, 0), COL_ATTN_K // LANES)),
            pl.BlockSpec((1, ATTN_BLOCK, LANES),
                         lambda bi, i: (bi, jnp.maximum(i * nb - 1, 0), COL_ATTN_V // LANES)),
            pl.BlockSpec((1, WIDTH), lambda bi, i: (0, 0)),
            pl.BlockSpec((1, LANES), lambda bi, i: (0, 0)),
            pl.BlockSpec((WIDTH, WIDTH), lambda bi, i: (0, 0)),
        ],
        out_specs=pl.BlockSpec((1, tq, WIDTH), lambda bi, i: (bi, i, 0)),
        out_shape=jax.ShapeDtypeStruct((b, t, WIDTH), BF16),
        compiler_params=_params("arbitrary", "arbitrary"),
        name="swa_attention",
    )(sinks, proj, proj, proj, proj, proj, gq, gk, bd)


def _ret_kernel(x_ref, cos_ref, sin_ref, dmat_ref, xi_ref, zeta_ref, cd_ref, bd_ref, o_ref, state_ref):
    @pl.when(pl.program_id(1) == 0)
    def _():
        state_ref[...] = jnp.zeros_like(state_ref)

    c = RET_CHUNK
    lane = lax.broadcasted_iota(jnp.int32, (c, LANES), 1)
    first_half = (lane & (HEAD - 1)) < HEAD // 2
    low = lane < HEAD
    rr = lax.broadcasted_iota(jnp.int32, (LANES, LANES), 0)
    cc = lax.broadcasted_iota(jnp.int32, (LANES, LANES), 1)
    same_head = (rr < HEAD) == (cc < HEAD)

    def rotary(x, p):
        swapped = jnp.where(first_half, pltpu.roll(x, LANES - HEAD // 2, 1), pltpu.roll(x, HEAD // 2, 1))
        sl = slice(p * LANES, (p + 1) * LANES)
        return x * cos_ref[:, sl] + swapped * sin_ref[:, sl]

    outs = []
    for p in range(N_PAIRS):
        sl = slice(p * LANES, (p + 1) * LANES)
        q = rotary(x_ref[0, :, p * LANES:(p + 1) * LANES].astype(F32), p)
        k = rotary(x_ref[0, :, WIDTH + p * LANES:WIDTH + (p + 1) * LANES].astype(F32), p) * HEAD ** -0.5
        v = x_ref[0, :, 2 * WIDTH + p * LANES:2 * WIDTH + (p + 1) * LANES]
        kb = k.astype(BF16)
        inner = jnp.zeros((c, LANES), F32)
        for e in range(2):
            in_e = low if e == 0 else jnp.logical_not(low)
            s = _dot(jnp.where(in_e, q, 0.0).astype(BF16), kb, NT) * dmat_ref[2 * p + e]
            inner = inner + _dot(s.astype(BF16), jnp.where(in_e, v, jnp.zeros_like(v)))
        state = state_ref[p]
        cross = _dot((q * xi_ref[:, sl]).astype(BF16), state.astype(BF16))
        kv = _dot((k * zeta_ref[:, sl]).astype(BF16), v, TN)
        state_ref[p] = state * cd_ref[p] + jnp.where(same_head, kv, 0.0)
        outs.append(inner + cross)
    o = jnp.concatenate(outs, axis=1)
    o = o * lax.rsqrt(_seg_sum(o * o, bd_ref[...]) * (1.0 / HEAD) + NORM_EPS)
    g = x_ref[0, :, 3 * WIDTH:4 * WIDTH].astype(F32)
    o_ref[0] = (o * (g * jax.nn.sigmoid(g))).astype(o_ref.dtype)


def _retention(proj, tables, bd):
    b, t, _ = proj.shape
    c = RET_CHUNK
    cos, sin, dmat, xi, zeta, cd = tables
    return pl.pallas_call(
        _ret_kernel,
        grid=(b, t // c),
        in_specs=[
            pl.BlockSpec((1, c, 4 * WIDTH), lambda bi, n: (bi, n, COL_RET // (4 * WIDTH))),
            pl.BlockSpec((c, WIDTH), lambda bi, n: (n, 0)),
            pl.BlockSpec((c, WIDTH), lambda bi, n: (n, 0)),
            pl.BlockSpec((N_HEADS, c, c), lambda bi, n: (0, 0, 0)),
            pl.BlockSpec((c, WIDTH), lambda bi, n: (0, 0)),
            pl.BlockSpec((c, WIDTH), lambda bi, n: (0, 0)),
            pl.BlockSpec((N_PAIRS, LANES, LANES), lambda bi, n: (0, 0, 0)),
            pl.BlockSpec((WIDTH, WIDTH), lambda bi, n: (0, 0)),
        ],
        out_specs=pl.BlockSpec((1, c, WIDTH), lambda bi, n: (bi, n, 0)),
        out_shape=jax.ShapeDtypeStruct((b, t, WIDTH), BF16),
        scratch_shapes=[pltpu.VMEM((N_PAIRS, LANES, LANES), F32)],
        compiler_params=_params("arbitrary", "arbitrary"),
        name="retention",
    )(proj, cos, sin, dmat, xi, zeta, cd, bd)


def _retention_tables(t):
    c = RET_CHUNK
    half = HEAD // 2
    inv_freq = 1.0 / (ROPE_BASE ** (jnp.arange(half, dtype=F32) * 2.0 / HEAD))
    ang = jnp.arange(t, dtype=F32)[:, None] * inv_freq[None, :]
    cos = jnp.tile(jnp.concatenate([jnp.cos(ang), jnp.cos(ang)], axis=1), (1, N_HEADS))
    sin = jnp.tile(jnp.concatenate([-jnp.sin(ang), jnp.sin(ang)], axis=1), (1, N_HEADS))
    log_gamma = jnp.log1p(-jnp.power(2.0, -5.0 - jnp.arange(N_HEADS, dtype=F32)))
    idx = jnp.arange(c, dtype=F32)
    diff = idx[:, None] - idx[None, :]
    dmat = jnp.where(diff >= 0, jnp.exp(log_gamma[:, None, None] * jnp.maximum(diff, 0.0)), 0.0)
    xi = jnp.exp(log_gamma[:, None] * (idx[None, :] + 1.0))
    zeta = jnp.exp(log_gamma[:, None] * (c - 1.0 - idx[None, :]))
    chunk_decay = jnp.exp(log_gamma * c)
    xi = jnp.repeat(xi.T, HEAD, axis=1)
    zeta = jnp.repeat(zeta.T, HEAD, axis=1)
    cd = jnp.broadcast_to(jnp.repeat(chunk_decay, HEAD).reshape(N_PAIRS, LANES, 1), (N_PAIRS, LANES, LANES))
    return cos, sin, dmat, xi, zeta, cd


def _rwkv_kernel(r_ref, k_ref, v_ref, lo_ref, mu_ref, w0_ref, a0_ref, kk_ref, ka_ref, rk_ref,
                 lng_ref, lnb_ref, wl_ref, tri_ref, bd_ref, o_ref, carry_ref, state_ref, y_ref):
    L = RWKV_CHUNK

    @pl.when(pl.program_id(1) == 0)
    def _():
        carry_ref[...] = jnp.zeros_like(carry_ref)
        state_ref[...] = jnp.zeros_like(state_ref)

    bd = bd_ref[...]

    def mix(ref, slot, width, mu):
        h = ref[0].astype(F32)
        row = lax.broadcasted_iota(jnp.int32, h.shape, 0)
        prev = jnp.where(row == 0, carry_ref[slot:slot + 1, :width], pltpu.roll(h, 1, 0))
        carry_ref[slot:slot + 1, :width] = h[L - 1:L]
        return h + mu * (prev - h)

    r = mix(r_ref, 0, WIDTH, mu_ref[:, 0:WIDTH])
    k = mix(k_ref, 1, WIDTH, mu_ref[:, WIDTH:2 * WIDTH])
    v = mix(v_ref, 2, WIDTH, mu_ref[:, 2 * WIDTH:3 * WIDTH])
    lo = mix(lo_ref, 3, 2 * LANES, mu_ref[:, 3 * WIDTH:3 * WIDTH + 2 * LANES])

    lane_lo = lax.broadcasted_iota(jnp.int32, lo.shape, 1)
    act = jnp.where(lane_lo < HEAD, jnp.tanh(lo), jnp.where(lane_lo < 2 * HEAD, lo, jax.nn.sigmoid(lo)))
    lora = _dot(act.astype(BF16), wl_ref[...])
    wx = -(w0_ref[...] + lora[:, 0:WIDTH])
    softplus = jnp.maximum(wx, 0.0) + jnp.log(1.0 + jnp.exp(-jnp.abs(wx)))
    lw = -jnp.exp(-softplus - 0.5)
    a = jax.nn.sigmoid(a0_ref[...] + lora[:, WIDTH:2 * WIDTH])
    g = lora[:, 2 * WIDTH:3 * WIDTH]
    kkr = k * kk_ref[...]
    kk = kkr / jnp.maximum(jnp.sqrt(_seg_sum(kkr * kkr, bd)), 1e-12)
    k2 = k * (1.0 + (a - 1.0) * ka_ref[...])

    cum = _dot(tri_ref[...], lw, precision=HI)
    e_pos = jnp.exp(cum)
    e_neg = jnp.exp(-cum)
    r_t = r * e_pos
    a_t = -kk * jnp.exp(cum - lw)
    b_t = kk * a * e_neg
    k_t = k2 * e_neg
    e_end = e_pos[L - 1:L]
    b_h = b_t * e_end
    k_h = k_t * e_end

    lane = lax.broadcasted_iota(jnp.int32, (L, LANES), 1)
    trow = lax.broadcasted_iota(jnp.int32, (L, LANES), 0)
    low = lane < HEAD
    tcol = lane & (HEAD - 1)
    strict = tcol < trow
    incl = tcol <= trow
    eye = (tcol == trow).astype(F32)
    rr = lax.broadcasted_iota(jnp.int32, (LANES, LANES), 0)
    cc = lax.broadcasted_iota(jnp.int32, (LANES, LANES), 1)
    same_head = (rr < HEAD) == (cc < HEAD)

    def bdiag(z):
        return jnp.concatenate([jnp.where(low, z, 0.0), jnp.where(low, 0.0, z)], axis=0)

    def bmm(x, z):
        return _dot(x, bdiag(z), precision=HI)

    for p in range(N_PAIRS):
        sl = slice(p * LANES, (p + 1) * LANES)
        x = jnp.concatenate([a_t[:, sl], r_t[:, sl]], axis=0)
        tb = _dot(x, bdiag(b_t[:, sl]), NT, precision=HI)
        tk = _dot(x, bdiag(k_t[:, sl]), NT, precision=HI)
        t_ab = jnp.where(strict, tb[:L], 0.0)
        t_rb = jnp.where(incl, tb[L:], 0.0)
        t_ak = jnp.where(strict, tk[:L], 0.0)
        t_rk = jnp.where(incl, tk[L:], 0.0)
        inv = eye + t_ab
        pw = t_ab
        for _ in range(5):
            pw = bmm(pw, pw)
            inv = inv + bmm(inv, pw)
        state = state_ref[p]
        xs = _dot(x, state, NT, precision=HI)
        vp = v[:, sl]
        u = bmm(inv, xs[:L] + bmm(t_ak, vp))
        y_ref[:, sl] = xs[L:] + bmm(t_rb, u) + bmm(t_rk, vp)
        upd = _dot(jnp.concatenate([u, vp], axis=0), jnp.concatenate([b_h[:, sl], k_h[:, sl]], axis=0),
                   TN, precision=HI)
        state_ref[p] = state * e_end[:, sl] + jnp.where(same_head, upd, 0.0)

    y = y_ref[...]
    mean = _seg_sum(y, bd) * (1.0 / HEAD)
    d = y - mean
    var = _seg_sum(d * d, bd) * (1.0 / HEAD)
    yn = d * lax.rsqrt(var + RWKV_LN_EPS) * lng_ref[...] + lnb_ref[...]
    bonus = _seg_sum(r * k2 * rk_ref[...], bd) * v
    o_ref[0] = ((yn + bonus) * g).astype(o_ref.dtype)


def _rwkv(proj, prm, tri, bd):
    b, t, _ = proj.shape
    L = RWKV_CHUNK
    row = lambda w: pl.BlockSpec((1, w), lambda bi, n: (0, 0))
    return pl.pallas_call(
        _rwkv_kernel,
        grid=(b, t // L),
        in_specs=[
            pl.BlockSpec((1, L, WIDTH), lambda bi, n: (bi, n, COL_RWKV_R // WIDTH)),
            pl.BlockSpec((1, L, WIDTH), lambda bi, n: (bi, n, COL_RWKV_K // WIDTH)),
            pl.BlockSpec((1, L, WIDTH), lambda bi, n: (bi, n, COL_RWKV_V // WIDTH)),
            pl.BlockSpec((1, L, 2 * LANES), lambda bi, n: (bi, n, COL_RWKV_LORA // (2 * LANES))),
            row(3 * WIDTH + 2 * LANES), row(WIDTH), row(WIDTH), row(WIDTH), row(WIDTH), row(WIDTH),
            row(WIDTH), row(WIDTH),
            pl.BlockSpec((2 * LANES, 3 * WIDTH), lambda bi, n: (0, 0)),
            pl.BlockSpec((L, L), lambda bi, n: (0, 0)),
            pl.BlockSpec((WIDTH, WIDTH), lambda bi, n: (0, 0)),
        ],
        out_specs=pl.BlockSpec((1, L, WIDTH), lambda bi, n: (bi, n, 0)),
        out_shape=jax.ShapeDtypeStruct((b, t, WIDTH), BF16),
        scratch_shapes=[
            pltpu.VMEM((8, WIDTH), F32),
            pltpu.VMEM((N_PAIRS, LANES, LANES), F32),
            pltpu.VMEM((L, WIDTH), F32),
        ],
        compiler_params=_params("arbitrary", "arbitrary"),
        name="rwkv7",
    )(proj, proj, proj, proj, *prm, tri, bd)


def _merge_kernel(oa_ref, ob_ref, oc_ref, g0_ref, g1_ref, g2_ref, x_ref, wa_ref, wb_ref, wc_ref, wo_ref, o_ref):
    def branch(o, w, g):
        return jax.nn.sigmoid(g[...].astype(F32)) * _dot(o[...], w[...])

    mixed = branch(oa_ref, wa_ref, g0_ref) + branch(ob_ref, wb_ref, g1_ref) + branch(oc_ref, wc_ref, g2_ref)
    o_ref[...] = x_ref[...] + _dot(mixed.astype(BF16), wo_ref[...])


def _merge(x, proj, oa, ob, oc, wa, wb, wc, wo, tm):
    n, d = x.shape
    branch = pl.BlockSpec((tm, WIDTH), lambda i: (i, 0))
    gate = lambda j: pl.BlockSpec((tm, d), lambda i: (i, COL_GATES // d + j))
    return pl.pallas_call(
        _merge_kernel,
        grid=(n // tm,),
        in_specs=[branch, branch, branch, gate(0), gate(1), gate(2),
                  pl.BlockSpec((tm, d), lambda i: (i, 0)),
                  _resident((WIDTH, d)), _resident((WIDTH, d)), _resident((WIDTH, d)), _resident((d, d))],
        out_specs=pl.BlockSpec((tm, d), lambda i: (i, 0)),
        out_shape=jax.ShapeDtypeStruct((n, d), F32),
        compiler_params=_params("arbitrary"),
        name="merge_out_proj",
    )(oa, ob, oc, proj, proj, proj, x, wa, wb, wc, wo)


def _ffn_kernel(x_ref, g_ref, wg_ref, wu_ref, wd_ref, o_ref):
    x = x_ref[...]
    ms = jnp.mean(x * x, axis=-1, keepdims=True)
    h = (x * lax.rsqrt(ms + NORM_EPS) * g_ref[...]).astype(BF16)
    gate = _dot(h, wg_ref[...])
    up = _dot(h, wu_ref[...])
    act = (gate * jax.nn.sigmoid(gate) * up).astype(BF16)
    o_ref[...] = x + _dot(act, wd_ref[...])


def _ffn(x, g, wg, wu, wd, tm):
    n, d = x.shape
    f = wg.shape[1]
    return pl.pallas_call(
        _ffn_kernel,
        grid=(n // tm,),
        in_specs=[pl.BlockSpec((tm, d), lambda i: (i, 0)), _resident((1, d)),
                  _resident((d, f)), _resident((d, f)), _resident((f, d))],
        out_specs=pl.BlockSpec((tm, d), lambda i: (i, 0)),
        out_shape=jax.ShapeDtypeStruct((n, d), F32),
        compiler_params=_params("arbitrary"),
        name="swiglu_ffn",
    )(x, g, wg, wu, wd)


def _permute_w_in(w):
    aq, ak, av = w[:, 0:512], w[:, 512:640], w[:, 640:768]
    rw = w[:, 768:2560]
    ret = w[:, 2560:4608]
    gates = w[:, 4608:7680]
    return jnp.concatenate(
        [ret, gates, rw[:, 0:512], rw[:, 512:1024], rw[:, 1024:1536], aq, rw[:, 1536:1792], ak, av], axis=1)


def _lora_weight(w2, a2, g2):
    w = jnp.zeros((2 * LANES, 3 * WIDTH), F32)
    w = w.at[0:64, 0:WIDTH].set(w2)
    w = w.at[64:128, WIDTH:2 * WIDTH].set(a2)
    w = w.at[128:256, 2 * WIDTH:3 * WIDTH].set(g2)
    return w.astype(BF16)


def _layer(x, consts, norm1_g, w_in, attn_q_norm_g, attn_k_norm_g, attn_sinks, w_attn_o,
           rwkv_shift_mu, rwkv_w0, rwkv_w2, rwkv_a0, rwkv_a2, rwkv_g2, rwkv_k_k, rwkv_k_a,
           rwkv_r_k, rwkv_lnx_g, rwkv_lnx_b, w_rwkv_o, w_ret_o, w_out,
           norm2_g, w_ffn_gate, w_ffn_up, w_ffn_down):
    b, t, d = x.shape
    n = b * t
    bd, tri, ret_tables = consts
    x2 = x.reshape(n, d)
    proj = _rms_matmul(x2, norm1_g.reshape(1, d), _permute_w_in(w_in).astype(BF16), tm=1024, tn=1536)
    proj3 = proj.reshape(b, t, IN_WIDTH)

    o_a = _attention(proj3, attn_sinks, jnp.tile(attn_q_norm_g, N_HEADS).reshape(1, WIDTH),
                     jnp.tile(attn_k_norm_g, 2).reshape(1, LANES), bd, tq=512)

    row = lambda a: a.reshape(1, WIDTH)
    prm = (rwkv_shift_mu.reshape(1, -1), row(rwkv_w0), row(rwkv_a0), row(rwkv_k_k), row(rwkv_k_a), row(rwkv_r_k),
           row(rwkv_lnx_g), row(rwkv_lnx_b), _lora_weight(rwkv_w2, rwkv_a2, rwkv_g2))
    o_b = _rwkv(proj3, prm, tri, bd)

    o_c = _retention(proj3, ret_tables, bd)

    x2 = _merge(x2, proj, o_a.reshape(n, WIDTH), o_b.reshape(n, WIDTH), o_c.reshape(n, WIDTH),
                w_attn_o.astype(BF16), w_rwkv_o.astype(BF16), w_ret_o.astype(BF16), w_out.astype(BF16), tm=512)
    x2 = _ffn(x2, norm2_g.reshape(1, d), w_ffn_gate.astype(BF16), w_ffn_up.astype(BF16),
              w_ffn_down.astype(BF16), tm=512)
    return x2.reshape(b, t, d)


def _constants(t):
    head_id = jnp.arange(WIDTH) // HEAD
    bd = (head_id[:, None] == head_id[None, :]).astype(BF16)
    idx = jnp.arange(RWKV_CHUNK)
    tri = (idx[None, :] <= idx[:, None]).astype(F32)
    return bd, tri, _retention_tables(t)


def kernel(x, norm1_g, w_in, attn_q_norm_g, attn_k_norm_g, attn_sinks, w_attn_o, rwkv_shift_mu, rwkv_w0, rwkv_w2, rwkv_a0, rwkv_a2, rwkv_g2, rwkv_k_k, rwkv_k_a, rwkv_r_k, rwkv_lnx_g, rwkv_lnx_b, w_rwkv_o, w_ret_o, w_out, norm2_g, w_ffn_gate, w_ffn_up, w_ffn_down):
    consts = _constants(x.shape[1])
    weights = (norm1_g, w_in, attn_q_norm_g, attn_k_norm_g, attn_sinks, w_attn_o, rwkv_shift_mu, rwkv_w0,
               rwkv_w2, rwkv_a0, rwkv_a2, rwkv_g2, rwkv_k_k, rwkv_k_a, rwkv_r_k, rwkv_lnx_g, rwkv_lnx_b,
               w_rwkv_o, w_ret_o, w_out, norm2_g, w_ffn_gate, w_ffn_up, w_ffn_down)
    for layer in range(norm1_g.shape[0]):
        x = _layer(x, consts, *(w[layer] for w in weights))
    return x
```

```python
import functools

import jax
import jax.numpy as jnp
from jax import lax
from jax.experimental import pallas as pl
from jax.experimental.pallas import tpu as pltpu

F32 = jnp.float32
BF16 = jnp.bfloat16

HEAD = 64
LANES = 128
D_MODEL = 1024
N_HEADS = 8
WIDTH = N_HEADS * HEAD
N_PAIRS = WIDTH // LANES
ATTN_BLOCK = 128
RET_CHUNK = 128
RWKV_CHUNK = 64
RWKV_BLOCK = 256
NORM_EPS = 1e-6
RWKV_LN_EPS = 64e-5
ROPE_BASE = 10000.0
VMEM_LIMIT = 56 * 1024 * 1024

COL_RET = 0
COL_GATES = 2048
COL_RWKV_R = 5120
COL_RWKV_K = 5632
COL_RWKV_V = 6144
COL_ATTN_Q = 6656
COL_RWKV_LORA = 7168
COL_ATTN_K = 7424
COL_ATTN_V = 7552
IN_WIDTH = 7680

NN = (((1,), (0,)), ((), ()))
NT = (((1,), (1,)), ((), ()))
TN = (((0,), (0,)), ((), ()))
HI = lax.Precision.HIGHEST


def _dot(a, b, dims=NN, precision=None):
    return lax.dot_general(a, b, dims, precision=precision, preferred_element_type=F32)


def _seg_sum(x, bd):
    hi = x.astype(BF16)
    lo = (x - hi.astype(F32)).astype(BF16)
    return _dot(hi, bd) + _dot(lo, bd)


def _params(*sem):
    return pltpu.CompilerParams(dimension_semantics=sem, vmem_limit_bytes=VMEM_LIMIT)


def _resident(shape):
    nd = len(shape)
    return pl.BlockSpec(shape, lambda *_: (0,) * nd, pipeline_mode=pl.Buffered(1))


def _rms_matmul_kernel(x_ref, g_ref, w_ref, o_ref, h_ref):
    @pl.when(pl.program_id(1) == 0)
    def _():
        x = x_ref[...]
        ms = jnp.mean(x * x, axis=-1, keepdims=True)
        h_ref[...] = (x * lax.rsqrt(ms + NORM_EPS) * g_ref[...]).astype(BF16)

    o_ref[...] = _dot(h_ref[...], w_ref[...]).astype(o_ref.dtype)


def _rms_matmul(x, g, w, tm, tn):
    n, d = x.shape
    cols = w.shape[1]
    return pl.pallas_call(
        _rms_matmul_kernel,
        grid=(n // tm, cols // tn),
        in_specs=[
            pl.BlockSpec((tm, d), lambda i, j: (i, 0)),
            pl.BlockSpec((1, d), lambda i, j: (0, 0)),
            pl.BlockSpec((d, tn), lambda i, j: (0, j)),
        ],
        out_specs=pl.BlockSpec((tm, tn), lambda i, j: (i, j)),
        out_shape=jax.ShapeDtypeStruct((n, cols), BF16),
        scratch_shapes=[pltpu.VMEM((tm, d), BF16)],
        compiler_params=_params("arbitrary", "arbitrary"),
        name="in_proj",
    )(x, g, w)


def _attn_kernel(sink_ref, q_ref, kc_ref, vc_ref, kp_ref, vp_ref, gq_ref, gk_ref, bd_ref, o_ref, *, tq):
    i = pl.program_id(1)
    bd = bd_ref[...]
    q = q_ref[0].astype(F32)
    qn = q * lax.rsqrt(_seg_sum(q * q, bd) * (1.0 / HEAD) + NORM_EPS) * (gq_ref[...] * HEAD ** -0.5)
    k = jnp.concatenate([kp_ref[0], kc_ref[0]], axis=0).astype(F32)
    kn = k * lax.rsqrt(_seg_sum(k * k, bd[:LANES, :LANES]) * (1.0 / HEAD) + NORM_EPS) * gk_ref[...]
    v = jnp.concatenate([vp_ref[0], vc_ref[0]], axis=0).astype(F32)

    lane_k = lax.broadcasted_iota(jnp.int32, k.shape, 1)
    lane_q = lax.broadcasted_iota(jnp.int32, (ATTN_BLOCK, LANES), 1)
    low_q = lane_q < HEAD
    row = lax.broadcasted_iota(jnp.int32, (2 * ATTN_BLOCK, 2 * ATTN_BLOCK), 0)
    col = lax.broadcasted_iota(jnp.int32, (2 * ATTN_BLOCK, 2 * ATTN_BLOCK), 1)
    rel = (row & (ATTN_BLOCK - 1)) + ATTN_BLOCK - col
    band = (rel >= 0) & (rel < ATTN_BLOCK)
    row1 = lax.broadcasted_iota(jnp.int32, (2 * ATTN_BLOCK, 1), 0)

    for e in range(2):
        in_e = (lane_k >= e * HEAD) & (lane_k < (e + 1) * HEAD)
        k_e = jnp.where(in_e, kn, 0.0)
        k_dup = (k_e + pltpu.roll(k_e, HEAD, 1)).astype(BF16)
        v_e = jnp.where(in_e, v, 0.0)
        v_lo = v_e if e == 0 else pltpu.roll(v_e, HEAD, 1)
        v_hi = pltpu.roll(v_lo, HEAD, 1)
        v_lo = v_lo.astype(BF16)
        v_hi = v_hi.astype(BF16)
        for n in range(tq // ATTN_BLOCK):
            r0 = n * ATTN_BLOCK
            first = (i * (tq // ATTN_BLOCK) + n) == 0
            valid = band & (col >= jnp.where(first, ATTN_BLOCK, 0))
            k_blk = k_dup[r0:r0 + 2 * ATTN_BLOCK]
            for c in (2 * e, 2 * e + 1):
                qc = qn[r0:r0 + ATTN_BLOCK, c * LANES:(c + 1) * LANES]
                qs = jnp.concatenate([jnp.where(low_q, qc, 0.0), jnp.where(low_q, 0.0, qc)], axis=0)
                s = _dot(qs.astype(BF16), k_blk, NT)
                s = jnp.where(valid, s, -jnp.inf)
                sink = jnp.where(row1 < ATTN_BLOCK, sink_ref[2 * c], sink_ref[2 * c + 1])
                m = jnp.maximum(jnp.max(s, axis=-1, keepdims=True), sink)
                p = jnp.exp(s - m)
                den = jnp.sum(p, axis=-1, keepdims=True) + jnp.exp(sink - m)
                pb = p.astype(BF16)
                o = (_dot(pb[:ATTN_BLOCK], v_lo[r0:r0 + 2 * ATTN_BLOCK])
                     + _dot(pb[ATTN_BLOCK:], v_hi[r0:r0 + 2 * ATTN_BLOCK]))
                inv = 1.0 / den
                o = o * jnp.where(low_q, inv[:ATTN_BLOCK], inv[ATTN_BLOCK:])
                o_ref[0, r0:r0 + ATTN_BLOCK, c * LANES:(c + 1) * LANES] = o.astype(o_ref.dtype)


def _attention(proj, sinks, gq, gk, bd, tq):
    b, t, _ = proj.shape
    nb = tq // ATTN_BLOCK
    kernel = functools.partial(_attn_kernel, tq=tq)
    return pl.pallas_call(
        kernel,
        grid=(b, t // tq),
        in_specs=[
            pl.BlockSpec(memory_space=pltpu.SMEM),
            pl.BlockSpec((1, tq, WIDTH), lambda bi, i: (bi, i, COL_ATTN_Q // WIDTH)),
            pl.BlockSpec((1, tq, LANES), lambda bi, i: (bi, i, COL_ATTN_K // LANES)),
            pl.BlockSpec((1, tq, LANES), lambda bi, i: (bi, i, COL_ATTN_V // LANES)),
            pl.BlockSpec((1, ATTN_BLOCK, LANES),
                         lambda bi, i: (bi, jnp.maximum(i * nb - 1, 0), COL_ATTN_K // LANES)),
            pl.BlockSpec((1, ATTN_BLOCK, LANES),
                         lambda bi, i: (bi, jnp.maximum(i * nb - 1, 0), COL_ATTN_V // LANES)),
            pl.BlockSpec((1, WIDTH), lambda bi, i: (0, 0)),
            pl.BlockSpec((1, LANES), lambda bi, i: (0, 0)),
            pl.BlockSpec((WIDTH, WIDTH), lambda bi, i: (0, 0)),
        ],
        out_specs=pl.BlockSpec((1, tq, WIDTH), lambda bi, i: (bi, i, 0)),
        out_shape=jax.ShapeDtypeStruct((b, t, WIDTH), BF16),
        compiler_params=_params("arbitrary", "arbitrary"),
        name="swa_attention",
    )(sinks, proj, proj, proj, proj, proj, gq, gk, bd)


def _ret_kernel(x_ref, cos_ref, sin_ref, dmat_ref, xi_ref, zeta_ref, cd_ref, bd_ref, o_ref, state_ref):
    @pl.when(pl.program_id(1) == 0)
    def _():
        state_ref[...] = jnp.zeros_like(state_ref)

    c = RET_CHUNK
    lane = lax.broadcasted_iota(jnp.int32, (c, LANES), 1)
    first_half = (lane & (HEAD - 1)) < HEAD // 2
    low = lane < HEAD
    rr = lax.broadcasted_iota(jnp.int32, (LANES, LANES), 0)
    cc = lax.broadcasted_iota(jnp.int32, (LANES, LANES), 1)
    same_head = (rr < HEAD) == (cc < HEAD)

    def rotary(x, p):
        swapped = jnp.where(first_half, pltpu.roll(x, LANES - HEAD // 2, 1), pltpu.roll(x, HEAD // 2, 1))
        sl = slice(p * LANES, (p + 1) * LANES)
        return x * cos_ref[:, sl] + swapped * sin_ref[:, sl]

    outs = []
    for p in range(N_PAIRS):
        sl = slice(p * LANES, (p + 1) * LANES)
        q = rotary(x_ref[0, :, p * LANES:(p + 1) * LANES].astype(F32), p)
        k = rotary(x_ref[0, :, WIDTH + p * LANES:WIDTH + (p + 1) * LANES].astype(F32), p) * HEAD ** -0.5
        v = x_ref[0, :, 2 * WIDTH + p * LANES:2 * WIDTH + (p + 1) * LANES]
        kb = k.astype(BF16)
        inner = jnp.zeros((c, LANES), F32)
        for e in range(2):
            in_e = low if e == 0 else jnp.logical_not(low)
            s = _dot(jnp.where(in_e, q, 0.0).astype(BF16), kb, NT) * dmat_ref[2 * p + e]
            inner = inner + _dot(s.astype(BF16), jnp.where(in_e, v, jnp.zeros_like(v)))
        state = state_ref[p]
        cross = _dot((q * xi_ref[:, sl]).astype(BF16), state.astype(BF16))
        kv = _dot((k * zeta_ref[:, sl]).astype(BF16), v, TN)
        state_ref[p] = state * cd_ref[p] + jnp.where(same_head, kv, 0.0)
        outs.append(inner + cross)
    o = jnp.concatenate(outs, axis=1)
    o = o * lax.rsqrt(_seg_sum(o * o, bd_ref[...]) * (1.0 / HEAD) + NORM_EPS)
    g = x_ref[0, :, 3 * WIDTH:4 * WIDTH].astype(F32)
    o_ref[0] = (o * (g * jax.nn.sigmoid(g))).astype(o_ref.dtype)


def _retention(proj, tables, bd):
    b, t, _ = proj.shape
    c = RET_CHUNK
    cos, sin, dmat, xi, zeta, cd = tables
    return pl.pallas_call(
        _ret_kernel,
        grid=(b, t // c),
        in_specs=[
            pl.BlockSpec((1, c, 4 * WIDTH), lambda bi, n: (bi, n, COL_RET // (4 * WIDTH))),
            pl.BlockSpec((c, WIDTH), lambda bi, n: (n, 0)),
            pl.BlockSpec((c, WIDTH), lambda bi, n: (n, 0)),
            pl.BlockSpec((N_HEADS, c, c), lambda bi, n: (0, 0, 0)),
            pl.BlockSpec((c, WIDTH), lambda bi, n: (0, 0)),
            pl.BlockSpec((c, WIDTH), lambda bi, n: (0, 0)),
            pl.BlockSpec((N_PAIRS, LANES, LANES), lambda bi, n: (0, 0, 0)),
            pl.BlockSpec((WIDTH, WIDTH), lambda bi, n: (0, 0)),
        ],
        out_specs=pl.BlockSpec((1, c, WIDTH), lambda bi, n: (bi, n, 0)),
        out_shape=jax.ShapeDtypeStruct((b, t, WIDTH), BF16),
        scratch_shapes=[pltpu.VMEM((N_PAIRS, LANES, LANES), F32)],
        compiler_params=_params("arbitrary", "arbitrary"),
        name="retention",
    )(proj, cos, sin, dmat, xi, zeta, cd, bd)


def _retention_tables(t):
    c = RET_CHUNK
    half = HEAD // 2
    inv_freq = 1.0 / (ROPE_BASE ** (jnp.arange(half, dtype=F32) * 2.0 / HEAD))
    ang = jnp.arange(t, dtype=F32)[:, None] * inv_freq[None, :]
    cos = jnp.tile(jnp.concatenate([jnp.cos(ang), jnp.cos(ang)], axis=1), (1, N_HEADS))
    sin = jnp.tile(jnp.concatenate([-jnp.sin(ang), jnp.sin(ang)], axis=1), (1, N_HEADS))
    log_gamma = jnp.log1p(-jnp.power(2.0, -5.0 - jnp.arange(N_HEADS, dtype=F32)))
    idx = jnp.arange(c, dtype=F32)
    diff = idx[:, None] - idx[None, :]
    dmat = jnp.where(diff >= 0, jnp.exp(log_gamma[:, None, None] * jnp.maximum(diff, 0.0)), 0.0)
    xi = jnp.exp(log_gamma[:, None] * (idx[None, :] + 1.0))
    zeta = jnp.exp(log_gamma[:, None] * (c - 1.0 - idx[None, :]))
    chunk_decay = jnp.exp(log_gamma * c)
    xi = jnp.repeat(xi.T, HEAD, axis=1)
    zeta = jnp.repeat(zeta.T, HEAD, axis=1)
    cd = jnp.broadcast_to(jnp.repeat(chunk_decay, HEAD).reshape(N_PAIRS, LANES, 1), (N_PAIRS, LANES, LANES))
    return cos, sin, dmat, xi, zeta, cd


def _bdot(a, b, dims=NN):
    return _dot(a.astype(BF16), b.astype(BF16), dims)


def _rwkv_kernel(r_ref, k_ref, v_ref, lo_ref, mu_ref, w0_ref, a0_ref, kk_ref, ka_ref, rk_ref,
                 lng_ref, lnb_ref, wl_ref, tri_ref, bd_ref, o_ref,
                 carry_ref, state_ref, y_ref):
    L = RWKV_CHUNK
    tb = r_ref.shape[1]
    nc = tb // L

    @pl.when(pl.program_id(1) == 0)
    def _():
        carry_ref[...] = jnp.zeros_like(carry_ref)
        state_ref[...] = jnp.zeros_like(state_ref)

    bd = bd_ref[...]

    def mix(ref, slot, width, mu):
        h = ref[0].astype(F32)
        row = lax.broadcasted_iota(jnp.int32, h.shape, 0)
        prev = jnp.where(row == 0, carry_ref[slot:slot + 1, :width], pltpu.roll(h, 1, 0))
        carry_ref[slot:slot + 1, :width] = h[tb - 1:tb]
        return h + mu * (prev - h)

    r = mix(r_ref, 0, WIDTH, mu_ref[:, 0:WIDTH])
    k = mix(k_ref, 1, WIDTH, mu_ref[:, WIDTH:2 * WIDTH])
    v = mix(v_ref, 2, WIDTH, mu_ref[:, 2 * WIDTH:3 * WIDTH])
    lo = mix(lo_ref, 3, 2 * LANES, mu_ref[:, 3 * WIDTH:3 * WIDTH + 2 * LANES])

    lane_lo = lax.broadcasted_iota(jnp.int32, lo.shape, 1)
    act = jnp.where(lane_lo < HEAD, jnp.tanh(lo), jnp.where(lane_lo < 2 * HEAD, lo, jax.nn.sigmoid(lo)))
    lora = _dot(act.astype(BF16), wl_ref[...])
    wx = -(w0_ref[...] + lora[:, 0:WIDTH])
    softplus = jnp.maximum(wx, 0.0) + jnp.log(1.0 + jnp.exp(-jnp.abs(wx)))
    lw = -jnp.exp(-softplus - 0.5)
    a = jax.nn.sigmoid(a0_ref[...] + lora[:, WIDTH:2 * WIDTH])
    g = lora[:, 2 * WIDTH:3 * WIDTH]
    kkr = k * kk_ref[...]
    kk = kkr / jnp.maximum(jnp.sqrt(_seg_sum(kkr * kkr, bd)), 1e-12)
    k2 = k * (1.0 + (a - 1.0) * ka_ref[...])

    lw_hi = lw.astype(BF16)
    lw_lo = (lw - lw_hi.astype(F32)).astype(BF16)
    cum = _dot(tri_ref[...], lw_hi) + _dot(tri_ref[...], lw_lo)
    e_pos = jnp.exp(cum)
    e_neg = jnp.exp(-cum)
    r_t = r * e_pos
    a_t = -kk * jnp.exp(cum - lw)
    b_t = kk * a * e_neg
    k_t = k2 * e_neg

    lane = lax.broadcasted_iota(jnp.int32, (L, LANES), 1)
    trow = lax.broadcasted_iota(jnp.int32, (L, LANES), 0)
    low = lane < HEAD
    tcol = lane & (HEAD - 1)
    strict = tcol < trow
    incl = tcol <= trow
    eye = (tcol == trow).astype(F32)
    rr = lax.broadcasted_iota(jnp.int32, (LANES, LANES), 0)
    cc = lax.broadcasted_iota(jnp.int32, (LANES, LANES), 1)
    same_head = (rr < HEAD) == (cc < HEAD)
    diag = rr == cc

    def bdiag(z):
        z = z.astype(BF16)
        zero = jnp.zeros_like(z)
        return jnp.concatenate([jnp.where(low, z, zero), jnp.where(low, zero, z)], axis=0)

    def bmm(x, z):
        return _dot(x.astype(BF16), bdiag(z))

    tiles = [(c, p) for c in range(nc) for p in range(N_PAIRS)]

    def tile(arr, c, p):
        return arr[c * L:(c + 1) * L, p * LANES:(p + 1) * LANES]

    at_ = [tile(a_t, c, p) for c, p in tiles]
    rt_ = [tile(r_t, c, p) for c, p in tiles]
    bt_ = [tile(b_t, c, p) for c, p in tiles]
    kt_ = [tile(k_t, c, p) for c, p in tiles]
    vp_ = [tile(v, c, p) for c, p in tiles]
    x_ = [jnp.concatenate([a_, r_], axis=0).astype(BF16) for a_, r_ in zip(at_, rt_)]
    tbm = [_dot(x, bdiag(b_), NT) for x, b_ in zip(x_, bt_)]
    tkm = [_dot(x, bdiag(k_), NT) for x, k_ in zip(x_, kt_)]
    t_ab = [jnp.where(strict, m[:L], 0.0) for m in tbm]
    t_rb = [jnp.where(incl, m[L:], 0.0) for m in tbm]
    t_ak = [jnp.where(strict, m[:L], 0.0) for m in tkm]
    t_rk = [jnp.where(incl, m[L:], 0.0) for m in tkm]
    inv = [eye + t for t in t_ab]
    pw = t_ab
    for _ in range(5):
        pw = [bmm(w, w) for w in pw]
        inv = [i + bmm(i, w) for i, w in zip(inv, pw)]
    a2 = [bmm(i, a_) for i, a_ in zip(inv, at_)]
    tv = [bmm(t, v_) for t, v_ in zip(t_ak, vp_)]
    uloc = [bmm(i, t) for i, t in zip(inv, tv)]
    qe = [r_ + bmm(t, a_) for r_, t, a_ in zip(rt_, t_rb, a2)]
    yl = [bmm(tb_, u) + bmm(tk_, v_) for tb_, u, tk_, v_ in zip(t_rb, uloc, t_rk, vp_)]
    e_end = [e_pos[(c + 1) * L - 1:(c + 1) * L, p * LANES:(p + 1) * LANES] for c, p in tiles]
    bh = [b_ * e for b_, e in zip(bt_, e_end)]
    kh = [k_ * e for k_, e in zip(kt_, e_end)]
    pm = [jnp.where(same_head, _bdot(a_, b_, TN), 0.0) + jnp.where(diag, e, 0.0)
          for a_, b_, e in zip(a2, bh, e_end)]
    gm = [jnp.where(same_head, _bdot(jnp.concatenate([u, v_], axis=0), jnp.concatenate([b_, k_], axis=0), TN), 0.0)
          for u, v_, b_, k_ in zip(uloc, vp_, bh, kh)]

    state = [state_ref[p] for p in range(N_PAIRS)]
    for c in range(nc):
        for p in range(N_PAIRS):
            n = c * N_PAIRS + p
            y_ref[c * L:(c + 1) * L, p * LANES:(p + 1) * LANES] = _bdot(qe[n], state[p], NT) + yl[n]
        state = [_bdot(state[p], pm[c * N_PAIRS + p]) + gm[c * N_PAIRS + p] for p in range(N_PAIRS)]
    for p in range(N_PAIRS):
        state_ref[p] = state[p]

    y = y_ref[...]
    mean = _seg_sum(y, bd) * (1.0 / HEAD)
    d = y - mean
    var = _seg_sum(d * d, bd) * (1.0 / HEAD)
    yn = d * lax.rsqrt(var + RWKV_LN_EPS) * lng_ref[...] + lnb_ref[...]
    bonus = _seg_sum(r * k2 * rk_ref[...], bd) * v
    o_ref[0] = ((yn + bonus) * g).astype(o_ref.dtype)


def _rwkv(proj, prm, tri, bd):
    b, t, _ = proj.shape
    tb = RWKV_BLOCK
    nc = tb // RWKV_CHUNK
    row = lambda w: pl.BlockSpec((1, w), lambda bi, n: (0, 0))
    return pl.pallas_call(
        _rwkv_kernel,
        grid=(b, t // tb),
        in_specs=[
            pl.BlockSpec((1, tb, WIDTH), lambda bi, n: (bi, n, COL_RWKV_R // WIDTH)),
            pl.BlockSpec((1, tb, WIDTH), lambda bi, n: (bi, n, COL_RWKV_K // WIDTH)),
            pl.BlockSpec((1, tb, WIDTH), lambda bi, n: (bi, n, COL_RWKV_V // WIDTH)),
            pl.BlockSpec((1, tb, 2 * LANES), lambda bi, n: (bi, n, COL_RWKV_LORA // (2 * LANES))),
            row(3 * WIDTH + 2 * LANES), row(WIDTH), row(WIDTH), row(WIDTH), row(WIDTH), row(WIDTH),
            row(WIDTH), row(WIDTH),
            pl.BlockSpec((2 * LANES, 3 * WIDTH), lambda bi, n: (0, 0)),
            pl.BlockSpec((tb, tb), lambda bi, n: (0, 0)),
            pl.BlockSpec((WIDTH, WIDTH), lambda bi, n: (0, 0)),
        ],
        out_specs=pl.BlockSpec((1, tb, WIDTH), lambda bi, n: (bi, n, 0)),
        out_shape=jax.ShapeDtypeStruct((b, t, WIDTH), BF16),
        scratch_shapes=[
            pltpu.VMEM((8, WIDTH), F32),
            pltpu.VMEM((N_PAIRS, LANES, LANES), F32),
            pltpu.VMEM((tb, WIDTH), F32),
        ],
        compiler_params=_params("arbitrary", "arbitrary"),
        name="rwkv7",
    )(proj, proj, proj, proj, *prm, tri, bd)


def _merge_kernel(oa_ref, ob_ref, oc_ref, g0_ref, g1_ref, g2_ref, x_ref, wa_ref, wb_ref, wc_ref, wo_ref, o_ref):
    def branch(o, w, g):
        return jax.nn.sigmoid(g[...].astype(F32)) * _dot(o[...], w[...])

    mixed = branch(oa_ref, wa_ref, g0_ref) + branch(ob_ref, wb_ref, g1_ref) + branch(oc_ref, wc_ref, g2_ref)
    o_ref[...] = x_ref[...] + _dot(mixed.astype(BF16), wo_ref[...])


def _merge(x, proj, oa, ob, oc, wa, wb, wc, wo, tm):
    n, d = x.shape
    branch = pl.BlockSpec((tm, WIDTH), lambda i: (i, 0))
    gate = lambda j: pl.BlockSpec((tm, d), lambda i: (i, COL_GATES // d + j))
    return pl.pallas_call(
        _merge_kernel,
        grid=(n // tm,),
        in_specs=[branch, branch, branch, gate(0), gate(1), gate(2),
                  pl.BlockSpec((tm, d), lambda i: (i, 0)),
                  _resident((WIDTH, d)), _resident((WIDTH, d)), _resident((WIDTH, d)), _resident((d, d))],
        out_specs=pl.BlockSpec((tm, d), lambda i: (i, 0)),
        out_shape=jax.ShapeDtypeStruct((n, d), F32),
        compiler_params=_params("arbitrary"),
        name="merge_out_proj",
    )(oa, ob, oc, proj, proj, proj, x, wa, wb, wc, wo)


def _ffn_kernel(x_ref, g_ref, wg_ref, wu_ref, wd_ref, o_ref):
    x = x_ref[...]
    ms = jnp.mean(x * x, axis=-1, keepdims=True)
    h = (x * lax.rsqrt(ms + NORM_EPS) * g_ref[...]).astype(BF16)
    gate = _dot(h, wg_ref[...])
    up = _dot(h, wu_ref[...])
    act = (gate * jax.nn.sigmoid(gate) * up).astype(BF16)
    o_ref[...] = x + _dot(act, wd_ref[...])


def _ffn(x, g, wg, wu, wd, tm):
    n, d = x.shape
    f = wg.shape[1]
    return pl.pallas_call(
        _ffn_kernel,
        grid=(n // tm,),
        in_specs=[pl.BlockSpec((tm, d), lambda i: (i, 0)), _resident((1, d)),
                  _resident((d, f)), _resident((d, f)), _resident((f, d))],
        out_specs=pl.BlockSpec((tm, d), lambda i: (i, 0)),
        out_shape=jax.ShapeDtypeStruct((n, d), F32),
        compiler_params=_params("arbitrary"),
        name="swiglu_ffn",
    )(x, g, wg, wu, wd)


def _permute_w_in(w):
    aq, ak, av = w[:, 0:512], w[:, 512:640], w[:, 640:768]
    rw = w[:, 768:2560]
    ret = w[:, 2560:4608]
    gates = w[:, 4608:7680]
    return jnp.concatenate(
        [ret, gates, rw[:, 0:512], rw[:, 512:1024], rw[:, 1024:1536], aq, rw[:, 1536:1792], ak, av], axis=1)


def _lora_weight(w2, a2, g2):
    w = jnp.zeros((2 * LANES, 3 * WIDTH), F32)
    w = w.at[0:64, 0:WIDTH].set(w2)
    w = w.at[64:128, WIDTH:2 * WIDTH].set(a2)
    w = w.at[128:256, 2 * WIDTH:3 * WIDTH].set(g2)
    return w.astype(BF16)


def _layer(x, consts, norm1_g, w_in, attn_q_norm_g, attn_k_norm_g, attn_sinks, w_attn_o,
           rwkv_shift_mu, rwkv_w0, rwkv_w2, rwkv_a0, rwkv_a2, rwkv_g2, rwkv_k_k, rwkv_k_a,
           rwkv_r_k, rwkv_lnx_g, rwkv_lnx_b, w_rwkv_o, w_ret_o, w_out,
           norm2_g, w_ffn_gate, w_ffn_up, w_ffn_down):
    b, t, d = x.shape
    n = b * t
    bd, tri, ret_tables = consts
    x2 = x.reshape(n, d)
    proj = _rms_matmul(x2, norm1_g.reshape(1, d), _permute_w_in(w_in).astype(BF16), tm=1024, tn=1536)
    proj3 = proj.reshape(b, t, IN_WIDTH)

    o_a = _attention(proj3, attn_sinks, jnp.tile(attn_q_norm_g, N_HEADS).reshape(1, WIDTH),
                     jnp.tile(attn_k_norm_g, 2).reshape(1, LANES), bd, tq=512)

    row = lambda a: a.reshape(1, WIDTH)
    prm = (rwkv_shift_mu.reshape(1, -1), row(rwkv_w0), row(rwkv_a0), row(rwkv_k_k), row(rwkv_k_a), row(rwkv_r_k),
           row(rwkv_lnx_g), row(rwkv_lnx_b), _lora_weight(rwkv_w2, rwkv_a2, rwkv_g2))
    o_b = _rwkv(proj3, prm, tri, bd)

    o_c = _retention(proj3, ret_tables, bd)

    x2 = _merge(x2, proj, o_a.reshape(n, WIDTH), o_b.reshape(n, WIDTH), o_c.reshape(n, WIDTH),
                w_attn_o.astype(BF16), w_rwkv_o.astype(BF16), w_ret_o.astype(BF16), w_out.astype(BF16), tm=512)
    x2 = _ffn(x2, norm2_g.reshape(1, d), w_ffn_gate.astype(BF16), w_ffn_up.astype(BF16),
              w_ffn_down.astype(BF16), tm=512)
    return x2.reshape(b, t, d)


def _constants(t):
    head_id = jnp.arange(WIDTH) // HEAD
    bd = (head_id[:, None] == head_id[None, :]).astype(BF16)
    idx = jnp.arange(RWKV_BLOCK)
    chunk = idx // RWKV_CHUNK
    tri = ((idx[None, :] <= idx[:, None]) & (chunk[None, :] == chunk[:, None])).astype(BF16)
    return bd, tri, _retention_tables(t)


def kernel(x, norm1_g, w_in, attn_q_norm_g, attn_k_norm_g, attn_sinks, w_attn_o, rwkv_shift_mu, rwkv_w0, rwkv_w2, rwkv_a0, rwkv_a2, rwkv_g2, rwkv_k_k, rwkv_k_a, rwkv_r_k, rwkv_lnx_g, rwkv_lnx_b, w_rwkv_o, w_ret_o, w_out, norm2_g, w_ffn_gate, w_ffn_up, w_ffn_down):
    consts = _constants(x.shape[1])
    weights = (norm1_g, w_in, attn_q_norm_g, attn_k_norm_g, attn_sinks, w_attn_o, rwkv_shift_mu, rwkv_w0,
               rwkv_w2, rwkv_a0, rwkv_a2, rwkv_g2, rwkv_k_k, rwkv_k_a, rwkv_r_k, rwkv_lnx_g, rwkv_lnx_b,
               w_rwkv_o, w_ret_o, w_out, norm2_g, w_ffn_gate, w_ffn_up, w_ffn_down)
    for layer in range(norm1_g.shape[0]):
        x = _layer(x, consts, *(w[layer] for w in weights))
    return x
```

```python
import functools

import jax
import jax.numpy as jnp
from jax import lax
from jax.experimental import pallas as pl
from jax.experimental.pallas import tpu as pltpu

F32 = jnp.float32
BF16 = jnp.bfloat16

HEAD = 64
LANES = 128
D_MODEL = 1024
N_HEADS = 8
WIDTH = N_HEADS * HEAD
N_PAIRS = WIDTH // LANES
ATTN_BLOCK = 128
RET_CHUNK = 128
RET_BLOCK = 512
RWKV_CHUNK = 64
RWKV_BLOCK = 256
NORM_EPS = 1e-6
RWKV_LN_EPS = 64e-5
ROPE_BASE = 10000.0
VMEM_LIMIT = 56 * 1024 * 1024

COL_RET = 0
COL_GATES = 2048
COL_RWKV_R = 5120
COL_RWKV_K = 5632
COL_RWKV_V = 6144
COL_ATTN_Q = 6656
COL_RWKV_LORA = 7168
COL_ATTN_K = 7424
COL_ATTN_V = 7552
IN_WIDTH = 7680

NN = (((1,), (0,)), ((), ()))
NT = (((1,), (1,)), ((), ()))
TN = (((0,), (0,)), ((), ()))


def _dot(a, b, dims=NN, precision=None):
    return lax.dot_general(a, b, dims, precision=precision, preferred_element_type=F32)


def _seg_sum(x, bd, split=True):
    cols = []
    for j in range(x.shape[1] // LANES):
        xc = x[:, j * LANES:(j + 1) * LANES]
        hi = xc.astype(BF16)
        s = _dot(hi, bd)
        if split:
            s = s + _dot((xc - hi.astype(F32)).astype(BF16), bd)
        cols.append(s)
    return cols[0] if len(cols) == 1 else jnp.concatenate(cols, axis=1)


def _params(*sem):
    return pltpu.CompilerParams(dimension_semantics=sem, vmem_limit_bytes=VMEM_LIMIT)


def _resident(shape):
    nd = len(shape)
    return pl.BlockSpec(shape, lambda *_: (0,) * nd, pipeline_mode=pl.Buffered(1))


def _rms_matmul_kernel(x_ref, g_ref, w_ref, o_ref, *, tn):
    x = x_ref[...]
    ms = jnp.mean(x * x, axis=-1, keepdims=True)
    h = (x * lax.rsqrt(ms + NORM_EPS) * g_ref[...]).astype(BF16)
    for j in range(w_ref.shape[1] // tn):
        o_ref[:, j * tn:(j + 1) * tn] = _dot(h, w_ref[:, j * tn:(j + 1) * tn]).astype(o_ref.dtype)


def _rms_matmul(x, g, w, tm, tn):
    n, d = x.shape
    cols = w.shape[1]
    return pl.pallas_call(
        functools.partial(_rms_matmul_kernel, tn=tn),
        grid=(n // tm,),
        in_specs=[pl.BlockSpec((tm, d), lambda i: (i, 0)), _resident((1, d)), _resident((d, cols))],
        out_specs=pl.BlockSpec((tm, cols), lambda i: (i, 0)),
        out_shape=jax.ShapeDtypeStruct((n, cols), BF16),
        compiler_params=_params("arbitrary"),
        name="in_proj",
    )(x, g, w)


def _attn_kernel(sink_ref, q_ref, kc_ref, vc_ref, kp_ref, vp_ref, gq_ref, gk_ref, bd_ref, o_ref, *, tq):
    i = pl.program_id(1)
    bd = bd_ref[...]
    q = q_ref[0].astype(F32)
    qn = q * lax.rsqrt(_seg_sum(q * q, bd, split=False) * (1.0 / HEAD) + NORM_EPS) * (gq_ref[...] * HEAD ** -0.5)
    k = jnp.concatenate([kp_ref[0], kc_ref[0]], axis=0).astype(F32)
    kn = k * lax.rsqrt(_seg_sum(k * k, bd, split=False) * (1.0 / HEAD) + NORM_EPS) * gk_ref[...]
    v = jnp.concatenate([vp_ref[0], vc_ref[0]], axis=0).astype(F32)

    lane_k = lax.broadcasted_iota(jnp.int32, k.shape, 1)
    lane_q = lax.broadcasted_iota(jnp.int32, (ATTN_BLOCK, LANES), 1)
    low_q = lane_q < HEAD
    row = lax.broadcasted_iota(jnp.int32, (2 * ATTN_BLOCK, 2 * ATTN_BLOCK), 0)
    col = lax.broadcasted_iota(jnp.int32, (2 * ATTN_BLOCK, 2 * ATTN_BLOCK), 1)
    rel = (row & (ATTN_BLOCK - 1)) + ATTN_BLOCK - col
    band = (rel >= 0) & (rel < ATTN_BLOCK)
    row1 = lax.broadcasted_iota(jnp.int32, (2 * ATTN_BLOCK, 1), 0)

    for e in range(2):
        in_e = (lane_k >= e * HEAD) & (lane_k < (e + 1) * HEAD)
        k_e = jnp.where(in_e, kn, 0.0)
        k_dup = (k_e + pltpu.roll(k_e, HEAD, 1)).astype(BF16)
        v_e = jnp.where(in_e, v, 0.0)
        v_lo = v_e if e == 0 else pltpu.roll(v_e, HEAD, 1)
        v_hi = pltpu.roll(v_lo, HEAD, 1)
        v_lo = v_lo.astype(BF16)
        v_hi = v_hi.astype(BF16)
        for n in range(tq // ATTN_BLOCK):
            r0 = n * ATTN_BLOCK
            first = (i * (tq // ATTN_BLOCK) + n) == 0
            valid = band & (col >= jnp.where(first, ATTN_BLOCK, 0))
            k_blk = k_dup[r0:r0 + 2 * ATTN_BLOCK]
            for c in (2 * e, 2 * e + 1):
                qc = qn[r0:r0 + ATTN_BLOCK, c * LANES:(c + 1) * LANES]
                qs = jnp.concatenate([jnp.where(low_q, qc, 0.0), jnp.where(low_q, 0.0, qc)], axis=0)
                s = _dot(qs.astype(BF16), k_blk, NT)
                s = jnp.where(valid, s, -jnp.inf)
                sink = jnp.where(row1 < ATTN_BLOCK, sink_ref[2 * c], sink_ref[2 * c + 1])
                m = jnp.maximum(jnp.max(s, axis=-1, keepdims=True), sink)
                p = jnp.exp(s - m)
                den = jnp.sum(p, axis=-1, keepdims=True) + jnp.exp(sink - m)
                pb = p.astype(BF16)
                o = (_dot(pb[:ATTN_BLOCK], v_lo[r0:r0 + 2 * ATTN_BLOCK])
                     + _dot(pb[ATTN_BLOCK:], v_hi[r0:r0 + 2 * ATTN_BLOCK]))
                inv = 1.0 / den
                o = o * jnp.where(low_q, inv[:ATTN_BLOCK], inv[ATTN_BLOCK:])
                o_ref[0, r0:r0 + ATTN_BLOCK, c * LANES:(c + 1) * LANES] = o.astype(o_ref.dtype)


def _attention(proj, sinks, gq, gk, bd, tq):
    b, t, _ = proj.shape
    nb = tq // ATTN_BLOCK
    kernel = functools.partial(_attn_kernel, tq=tq)
    return pl.pallas_call(
        kernel,
        grid=(b, t // tq),
        in_specs=[
            pl.BlockSpec(memory_space=pltpu.SMEM),
            pl.BlockSpec((1, tq, WIDTH), lambda bi, i: (bi, i, COL_ATTN_Q // WIDTH)),
            pl.BlockSpec((1, tq, LANES), lambda bi, i: (bi, i, COL_ATTN_K // LANES)),
            pl.BlockSpec((1, tq, LANES), lambda bi, i: (bi, i, COL_ATTN_V // LANES)),
            pl.BlockSpec((1, ATTN_BLOCK, LANES),
                         lambda bi, i: (bi, jnp.maximum(i * nb - 1, 0), COL_ATTN_K // LANES)),
            pl.BlockSpec((1, ATTN_BLOCK, LANES),
                         lambda bi, i: (bi, jnp.maximum(i * nb - 1, 0), COL_ATTN_V // LANES)),
            pl.BlockSpec((1, WIDTH), lambda bi, i: (0, 0)),
            pl.BlockSpec((1, LANES), lambda bi, i: (0, 0)),
            pl.BlockSpec((LANES, LANES), lambda bi, i: (0, 0)),
        ],
        out_specs=pl.BlockSpec((1, tq, WIDTH), lambda bi, i: (bi, i, 0)),
        out_shape=jax.ShapeDtypeStruct((b, t, WIDTH), BF16),
        compiler_params=_params("arbitrary", "arbitrary"),
        name="swa_attention",
    )(sinks, proj, proj, proj, proj, proj, gq, gk, bd)


def _ret_kernel(x_ref, cos_ref, sin_ref, dmat_ref, xi_ref, zeta_ref, cd_ref, bd_ref, o_ref, state_ref, acc_ref):
    @pl.when(pl.program_id(1) == 0)
    def _():
        state_ref[...] = jnp.zeros_like(state_ref)

    c = RET_CHUNK
    nc = x_ref.shape[1] // c
    lane = lax.broadcasted_iota(jnp.int32, (c, LANES), 1)
    first_half = (lane & (HEAD - 1)) < HEAD // 2
    low = lane < HEAD
    rr = lax.broadcasted_iota(jnp.int32, (LANES, LANES), 0)
    cc = lax.broadcasted_iota(jnp.int32, (LANES, LANES), 1)
    same_head = (rr < HEAD) == (cc < HEAD)
    tiles = [(ci, p) for ci in range(nc) for p in range(N_PAIRS)]

    def rotary(col, ci, p):
        rows = slice(ci * c, (ci + 1) * c)
        x = x_ref[0, rows, col + p * LANES:col + (p + 1) * LANES].astype(F32)
        swapped = jnp.where(first_half, pltpu.roll(x, LANES - HEAD // 2, 1), pltpu.roll(x, HEAD // 2, 1))
        sl = slice(p * LANES, (p + 1) * LANES)
        return x * cos_ref[rows, sl] + swapped * sin_ref[rows, sl]

    def split_heads(x):
        zero = jnp.zeros_like(x)
        return jnp.concatenate([jnp.where(low, x, zero), jnp.where(low, zero, x)], axis=0)

    q = [rotary(0, ci, p) for ci, p in tiles]
    k = [rotary(WIDTH, ci, p) * HEAD ** -0.5 for ci, p in tiles]
    v = [x_ref[0, ci * c:(ci + 1) * c, 2 * WIDTH + p * LANES:2 * WIDTH + (p + 1) * LANES] for ci, p in tiles]
    s = [_dot(split_heads(q_.astype(BF16)), k_.astype(BF16), NT) * dmat_ref[p]
         for q_, k_, (ci, p) in zip(q, k, tiles)]
    inner = [_dot(jnp.concatenate([s_[:c], s_[c:]], axis=1).astype(BF16), split_heads(v_)) for s_, v_ in zip(s, v)]
    kv = [jnp.where(same_head, _dot((k_ * zeta_ref[:, p * LANES:(p + 1) * LANES]).astype(BF16), v_, TN), 0.0)
          for k_, v_, (ci, p) in zip(k, v, tiles)]
    entering = []
    state = [state_ref[p] for p in range(N_PAIRS)]
    for ci in range(nc):
        for p in range(N_PAIRS):
            entering.append(state[p])
            state[p] = state[p] * cd_ref[p] + kv[ci * N_PAIRS + p]
    for p in range(N_PAIRS):
        state_ref[p] = state[p]
    for n, (ci, p) in enumerate(tiles):
        cross = _dot((q[n] * xi_ref[:, p * LANES:(p + 1) * LANES]).astype(BF16), entering[n].astype(BF16))
        acc_ref[ci * c:(ci + 1) * c, p * LANES:(p + 1) * LANES] = inner[n] + cross
    o = acc_ref[...]
    o = o * lax.rsqrt(_seg_sum(o * o, bd_ref[...], split=False) * (1.0 / HEAD) + NORM_EPS)
    g = x_ref[0, :, 3 * WIDTH:4 * WIDTH].astype(F32)
    o_ref[0] = (o * (g * jax.nn.sigmoid(g))).astype(o_ref.dtype)


def _retention(proj, tables, bd):
    b, t, _ = proj.shape
    c = RET_CHUNK
    tb = RET_BLOCK
    cos, sin, dmat, xi, zeta, cd = tables
    return pl.pallas_call(
        _ret_kernel,
        grid=(b, t // tb),
        in_specs=[
            pl.BlockSpec((1, tb, 4 * WIDTH), lambda bi, n: (bi, n, COL_RET // (4 * WIDTH))),
            pl.BlockSpec((tb, WIDTH), lambda bi, n: (n, 0)),
            pl.BlockSpec((tb, WIDTH), lambda bi, n: (n, 0)),
            pl.BlockSpec((N_PAIRS, 2 * c, c), lambda bi, n: (0, 0, 0)),
            pl.BlockSpec((c, WIDTH), lambda bi, n: (0, 0)),
            pl.BlockSpec((c, WIDTH), lambda bi, n: (0, 0)),
            pl.BlockSpec((N_PAIRS, LANES, LANES), lambda bi, n: (0, 0, 0)),
            pl.BlockSpec((LANES, LANES), lambda bi, n: (0, 0)),
        ],
        out_specs=pl.BlockSpec((1, tb, WIDTH), lambda bi, n: (bi, n, 0)),
        out_shape=jax.ShapeDtypeStruct((b, t, WIDTH), BF16),
        scratch_shapes=[pltpu.VMEM((N_PAIRS, LANES, LANES), F32), pltpu.VMEM((tb, WIDTH), F32)],
        compiler_params=_params("arbitrary", "arbitrary"),
        name="retention",
    )(proj, cos, sin, dmat, xi, zeta, cd, bd)


def _retention_tables(t):
    c = RET_CHUNK
    half = HEAD // 2
    inv_freq = 1.0 / (ROPE_BASE ** (jnp.arange(half, dtype=F32) * 2.0 / HEAD))
    ang = jnp.arange(t, dtype=F32)[:, None] * inv_freq[None, :]
    cos = jnp.tile(jnp.concatenate([jnp.cos(ang), jnp.cos(ang)], axis=1), (1, N_HEADS))
    sin = jnp.tile(jnp.concatenate([-jnp.sin(ang), jnp.sin(ang)], axis=1), (1, N_HEADS))
    log_gamma = jnp.log1p(-jnp.power(2.0, -5.0 - jnp.arange(N_HEADS, dtype=F32)))
    idx = jnp.arange(c, dtype=F32)
    diff = idx[:, None] - idx[None, :]
    dmat = jnp.where(diff >= 0, jnp.exp(log_gamma[:, None, None] * jnp.maximum(diff, 0.0)), 0.0)
    xi = jnp.exp(log_gamma[:, None] * (idx[None, :] + 1.0))
    zeta = jnp.exp(log_gamma[:, None] * (c - 1.0 - idx[None, :]))
    chunk_decay = jnp.exp(log_gamma * c)
    xi = jnp.repeat(xi.T, HEAD, axis=1)
    zeta = jnp.repeat(zeta.T, HEAD, axis=1)
    cd = jnp.broadcast_to(jnp.repeat(chunk_decay, HEAD).reshape(N_PAIRS, LANES, 1), (N_PAIRS, LANES, LANES))
    dmat = dmat.reshape(N_PAIRS, 2 * c, c)
    return cos, sin, dmat, xi, zeta, cd


def _bdot(a, b, dims=NN):
    return _dot(a.astype(BF16), b.astype(BF16), dims)


def _rwkv_kernel(r_ref, k_ref, v_ref, lo_ref, mu_ref, w0_ref, a0_ref, kk_ref, ka_ref, rk_ref,
                 lng_ref, lnb_ref, wl_ref, tri_ref, bd_ref, o_ref,
                 carry_ref, state_ref, y_ref):
    L = RWKV_CHUNK
    tb = r_ref.shape[1]
    nc = tb // L

    @pl.when(pl.program_id(1) == 0)
    def _():
        carry_ref[...] = jnp.zeros_like(carry_ref)
        state_ref[...] = jnp.zeros_like(state_ref)

    bd = bd_ref[...]

    def mix(ref, slot, width, mu):
        h = ref[0].astype(F32)
        row = lax.broadcasted_iota(jnp.int32, h.shape, 0)
        prev = jnp.where(row == 0, carry_ref[slot:slot + 1, :width], pltpu.roll(h, 1, 0))
        carry_ref[slot:slot + 1, :width] = h[tb - 1:tb]
        return h + mu * (prev - h)

    r = mix(r_ref, 0, WIDTH, mu_ref[:, 0:WIDTH])
    k = mix(k_ref, 1, WIDTH, mu_ref[:, WIDTH:2 * WIDTH])
    v = mix(v_ref, 2, WIDTH, mu_ref[:, 2 * WIDTH:3 * WIDTH])
    lo = mix(lo_ref, 3, 2 * LANES, mu_ref[:, 3 * WIDTH:3 * WIDTH + 2 * LANES])

    lane_lo = lax.broadcasted_iota(jnp.int32, lo.shape, 1)
    act = jnp.where(lane_lo < HEAD, jnp.tanh(lo), jnp.where(lane_lo < 2 * HEAD, lo, jax.nn.sigmoid(lo)))
    lora = _dot(act.astype(BF16), wl_ref[...])
    wx = -(w0_ref[...] + lora[:, 0:WIDTH])
    softplus = jnp.maximum(wx, 0.0) + jnp.log(1.0 + jnp.exp(-jnp.abs(wx)))
    lw = -jnp.exp(-softplus - 0.5)
    a = jax.nn.sigmoid(a0_ref[...] + lora[:, WIDTH:2 * WIDTH])
    g = lora[:, 2 * WIDTH:3 * WIDTH]
    kkr = k * kk_ref[...]
    kk = kkr / jnp.maximum(jnp.sqrt(_seg_sum(kkr * kkr, bd, split=False)), 1e-12)
    k2 = k * (1.0 + (a - 1.0) * ka_ref[...])

    lw_hi = lw.astype(BF16)
    lw_lo = (lw - lw_hi.astype(F32)).astype(BF16)
    cum = _dot(tri_ref[...], lw_hi) + _dot(tri_ref[...], lw_lo)
    e_pos = jnp.exp(cum)
    e_neg = jnp.exp(-cum)
    r_t = r * e_pos
    a_t = -kk * jnp.exp(cum - lw)
    b_t = kk * a * e_neg
    k_t = k2 * e_neg

    lane = lax.broadcasted_iota(jnp.int32, (L, LANES), 1)
    trow = lax.broadcasted_iota(jnp.int32, (L, LANES), 0)
    low = lane < HEAD
    tcol = lane & (HEAD - 1)
    strict = tcol < trow
    incl = tcol <= trow
    eye = (tcol == trow).astype(F32)
    rr = lax.broadcasted_iota(jnp.int32, (LANES, LANES), 0)
    cc = lax.broadcasted_iota(jnp.int32, (LANES, LANES), 1)
    same_head = (rr < HEAD) == (cc < HEAD)
    diag = rr == cc

    def bdiag(z):
        z = z.astype(BF16)
        zero = jnp.zeros_like(z)
        return jnp.concatenate([jnp.where(low, z, zero), jnp.where(low, zero, z)], axis=0)

    def bmm(x, z):
        return _dot(x.astype(BF16), bdiag(z))

    tiles = [(c, p) for c in range(nc) for p in range(N_PAIRS)]

    def tile(arr, c, p):
        return arr[c * L:(c + 1) * L, p * LANES:(p + 1) * LANES]

    at_ = [tile(a_t, c, p) for c, p in tiles]
    rt_ = [tile(r_t, c, p) for c, p in tiles]
    bt_ = [tile(b_t, c, p) for c, p in tiles]
    kt_ = [tile(k_t, c, p) for c, p in tiles]
    vp_ = [tile(v, c, p) for c, p in tiles]
    def stack(top, bottom):
        return jnp.concatenate([top, bottom], axis=0).astype(BF16)

    tt = [_dot(stack(a_, r_), jnp.concatenate([bdiag(b_), bdiag(k_)], axis=0), NT)
          for a_, r_, b_, k_ in zip(at_, rt_, bt_, kt_)]
    t_ab = [jnp.where(strict, m[:L, :LANES], 0.0) for m in tt]
    t_rb = [jnp.where(incl, m[L:, :LANES], 0.0) for m in tt]
    t_ak = [jnp.where(strict, m[:L, LANES:], 0.0) for m in tt]
    t_rk = [jnp.where(incl, m[L:, LANES:], 0.0) for m in tt]
    inv = [eye + t for t in t_ab]
    pw = [bmm(t, t) for t in t_ab]
    for _ in range(4):
        st = [_dot(stack(w, i), bdiag(w)) for w, i in zip(pw, inv)]
        inv = [i + s[L:] for i, s in zip(inv, st)]
        pw = [s[:L] for s in st]
    inv = [i + bmm(i, w) for i, w in zip(inv, pw)]
    tvk = [_dot(stack(ta, tk), bdiag(v_)) for ta, tk, v_ in zip(t_ak, t_rk, vp_)]
    lhs = [stack(i, bmm(t, i)) for i, t in zip(inv, t_rb)]
    ra = [_dot(x, bdiag(a_)) for x, a_ in zip(lhs, at_)]
    ru = [_dot(x, bdiag(s[:L])) for x, s in zip(lhs, tvk)]
    qe = [r_ + s[L:] for r_, s in zip(rt_, ra)]
    yl = [s[L:] + s2[L:] for s, s2 in zip(ru, tvk)]
    e_end = [e_pos[(c + 1) * L - 1:(c + 1) * L, p * LANES:(p + 1) * LANES] for c, p in tiles]
    bh = [b_ * e for b_, e in zip(bt_, e_end)]
    kh = [k_ * e for k_, e in zip(kt_, e_end)]
    au = [_bdot(jnp.concatenate([s[:L], s2[:L]], axis=1), b_, TN) for s, s2, b_ in zip(ra, ru, bh)]
    pm = [jnp.where(same_head, s[:LANES], 0.0) + jnp.where(diag, e, 0.0) for s, e in zip(au, e_end)]
    gm = [jnp.where(same_head, s[LANES:] + _bdot(v_, k_, TN), 0.0) for s, v_, k_ in zip(au, vp_, kh)]

    state = [state_ref[p] for p in range(N_PAIRS)]
    for c in range(nc):
        for p in range(N_PAIRS):
            n = c * N_PAIRS + p
            y_ref[c * L:(c + 1) * L, p * LANES:(p + 1) * LANES] = _bdot(qe[n], state[p], NT) + yl[n]
        state = [_bdot(state[p], pm[c * N_PAIRS + p]) + gm[c * N_PAIRS + p] for p in range(N_PAIRS)]
    for p in range(N_PAIRS):
        state_ref[p] = state[p]

    y = y_ref[...]
    mean = _seg_sum(y, bd) * (1.0 / HEAD)
    d = y - mean
    var = _seg_sum(d * d, bd, split=False) * (1.0 / HEAD)
    yn = d * lax.rsqrt(var + RWKV_LN_EPS) * lng_ref[...] + lnb_ref[...]
    bonus = _seg_sum(r * k2 * rk_ref[...], bd) * v
    o_ref[0] = ((yn + bonus) * g).astype(o_ref.dtype)


def _rwkv(proj, prm, tri, bd):
    b, t, _ = proj.shape
    tb = RWKV_BLOCK
    nc = tb // RWKV_CHUNK
    row = lambda w: pl.BlockSpec((1, w), lambda bi, n: (0, 0))
    return pl.pallas_call(
        _rwkv_kernel,
        grid=(b, t // tb),
        in_specs=[
            pl.BlockSpec((1, tb, WIDTH), lambda bi, n: (bi, n, COL_RWKV_R // WIDTH)),
            pl.BlockSpec((1, tb, WIDTH), lambda bi, n: (bi, n, COL_RWKV_K // WIDTH)),
            pl.BlockSpec((1, tb, WIDTH), lambda bi, n: (bi, n, COL_RWKV_V // WIDTH)),
            pl.BlockSpec((1, tb, 2 * LANES), lambda bi, n: (bi, n, COL_RWKV_LORA // (2 * LANES))),
            row(3 * WIDTH + 2 * LANES), row(WIDTH), row(WIDTH), row(WIDTH), row(WIDTH), row(WIDTH),
            row(WIDTH), row(WIDTH),
            pl.BlockSpec((2 * LANES, 3 * WIDTH), lambda bi, n: (0, 0)),
            pl.BlockSpec((tb, tb), lambda bi, n: (0, 0)),
            pl.BlockSpec((LANES, LANES), lambda bi, n: (0, 0)),
        ],
        out_specs=pl.BlockSpec((1, tb, WIDTH), lambda bi, n: (bi, n, 0)),
        out_shape=jax.ShapeDtypeStruct((b, t, WIDTH), BF16),
        scratch_shapes=[
            pltpu.VMEM((8, WIDTH), F32),
            pltpu.VMEM((N_PAIRS, LANES, LANES), F32),
            pltpu.VMEM((tb, WIDTH), F32),
        ],
        compiler_params=_params("arbitrary", "arbitrary"),
        name="rwkv7",
    )(proj, proj, proj, proj, *prm, tri, bd)


def _merge_kernel(oa_ref, ob_ref, oc_ref, g0_ref, g1_ref, g2_ref, x_ref, wa_ref, wb_ref, wc_ref, wo_ref, o_ref):
    def branch(o, w, g):
        return jax.nn.sigmoid(g[...].astype(F32)) * _dot(o[...], w[...])

    mixed = branch(oa_ref, wa_ref, g0_ref) + branch(ob_ref, wb_ref, g1_ref) + branch(oc_ref, wc_ref, g2_ref)
    o_ref[...] = x_ref[...] + _dot(mixed.astype(BF16), wo_ref[...])


def _merge(x, proj, oa, ob, oc, wa, wb, wc, wo, tm):
    n, d = x.shape
    branch = pl.BlockSpec((tm, WIDTH), lambda i: (i, 0))
    gate = lambda j: pl.BlockSpec((tm, d), lambda i: (i, COL_GATES // d + j))
    return pl.pallas_call(
        _merge_kernel,
        grid=(n // tm,),
        in_specs=[branch, branch, branch, gate(0), gate(1), gate(2),
                  pl.BlockSpec((tm, d), lambda i: (i, 0)),
                  _resident((WIDTH, d)), _resident((WIDTH, d)), _resident((WIDTH, d)), _resident((d, d))],
        out_specs=pl.BlockSpec((tm, d), lambda i: (i, 0)),
        out_shape=jax.ShapeDtypeStruct((n, d), F32),
        compiler_params=_params("arbitrary"),
        name="merge_out_proj",
    )(oa, ob, oc, proj, proj, proj, x, wa, wb, wc, wo)


def _ffn_kernel(x_ref, g_ref, wg_ref, wu_ref, wd_ref, o_ref):
    x = x_ref[...]
    ms = jnp.mean(x * x, axis=-1, keepdims=True)
    h = (x * lax.rsqrt(ms + NORM_EPS) * g_ref[...]).astype(BF16)
    gate = _dot(h, wg_ref[...])
    up = _dot(h, wu_ref[...])
    act = (gate * jax.nn.sigmoid(gate) * up).astype(BF16)
    o_ref[...] = x + _dot(act, wd_ref[...])


def _ffn(x, g, wg, wu, wd, tm):
    n, d = x.shape
    f = wg.shape[1]
    return pl.pallas_call(
        _ffn_kernel,
        grid=(n // tm,),
        in_specs=[pl.BlockSpec((tm, d), lambda i: (i, 0)), _resident((1, d)),
                  _resident((d, f)), _resident((d, f)), _resident((f, d))],
        out_specs=pl.BlockSpec((tm, d), lambda i: (i, 0)),
        out_shape=jax.ShapeDtypeStruct((n, d), F32),
        compiler_params=_params("arbitrary"),
        name="swiglu_ffn",
    )(x, g, wg, wu, wd)


def _permute_w_in(w):
    aq, ak, av = w[:, 0:512], w[:, 512:640], w[:, 640:768]
    rw = w[:, 768:2560]
    ret = w[:, 2560:4608]
    gates = w[:, 4608:7680]
    return jnp.concatenate(
        [ret, gates, rw[:, 0:512], rw[:, 512:1024], rw[:, 1024:1536], aq, rw[:, 1536:1792], ak, av], axis=1)


def _lora_weight(w2, a2, g2):
    w = jnp.zeros((2 * LANES, 3 * WIDTH), F32)
    w = w.at[0:64, 0:WIDTH].set(w2)
    w = w.at[64:128, WIDTH:2 * WIDTH].set(a2)
    w = w.at[128:256, 2 * WIDTH:3 * WIDTH].set(g2)
    return w.astype(BF16)


def _layer(x, consts, norm1_g, w_in, attn_q_norm_g, attn_k_norm_g, attn_sinks, w_attn_o,
           rwkv_shift_mu, rwkv_w0, rwkv_w2, rwkv_a0, rwkv_a2, rwkv_g2, rwkv_k_k, rwkv_k_a,
           rwkv_r_k, rwkv_lnx_g, rwkv_lnx_b, w_rwkv_o, w_ret_o, w_out,
           norm2_g, w_ffn_gate, w_ffn_up, w_ffn_down):
    b, t, d = x.shape
    n = b * t
    bd, tri, ret_tables = consts
    x2 = x.reshape(n, d)
    proj = _rms_matmul(x2, norm1_g.reshape(1, d), _permute_w_in(w_in).astype(BF16), tm=512, tn=1536)
    proj3 = proj.reshape(b, t, IN_WIDTH)

    o_a = _attention(proj3, attn_sinks, jnp.tile(attn_q_norm_g, N_HEADS).reshape(1, WIDTH),
                     jnp.tile(attn_k_norm_g, 2).reshape(1, LANES), bd, tq=512)

    row = lambda a: a.reshape(1, WIDTH)
    prm = (rwkv_shift_mu.reshape(1, -1), row(rwkv_w0), row(rwkv_a0), row(rwkv_k_k), row(rwkv_k_a), row(rwkv_r_k),
           row(rwkv_lnx_g), row(rwkv_lnx_b), _lora_weight(rwkv_w2, rwkv_a2, rwkv_g2))
    o_b = _rwkv(proj3, prm, tri, bd)

    o_c = _retention(proj3, ret_tables, bd)

    x2 = _merge(x2, proj, o_a.reshape(n, WIDTH), o_b.reshape(n, WIDTH), o_c.reshape(n, WIDTH),
                w_attn_o.astype(BF16), w_rwkv_o.astype(BF16), w_ret_o.astype(BF16), w_out.astype(BF16), tm=512)
    x2 = _ffn(x2, norm2_g.reshape(1, d), w_ffn_gate.astype(BF16), w_ffn_up.astype(BF16),
              w_ffn_down.astype(BF16), tm=512)
    return x2.reshape(b, t, d)


def _constants(t):
    head_id = jnp.arange(LANES) // HEAD
    bd = (head_id[:, None] == head_id[None, :]).astype(BF16)
    idx = jnp.arange(RWKV_BLOCK)
    chunk = idx // RWKV_CHUNK
    tri = ((idx[None, :] <= idx[:, None]) & (chunk[None, :] == chunk[:, None])).astype(BF16)
    return bd, tri, _retention_tables(t)


def kernel(x, norm1_g, w_in, attn_q_norm_g, attn_k_norm_g, attn_sinks, w_attn_o, rwkv_shift_mu, rwkv_w0, rwkv_w2, rwkv_a0, rwkv_a2, rwkv_g2, rwkv_k_k, rwkv_k_a, rwkv_r_k, rwkv_lnx_g, rwkv_lnx_b, w_rwkv_o, w_ret_o, w_out, norm2_g, w_ffn_gate, w_ffn_up, w_ffn_down):
    consts = _constants(x.shape[1])
    weights = (norm1_g, w_in, attn_q_norm_g, attn_k_norm_g, attn_sinks, w_attn_o, rwkv_shift_mu, rwkv_w0,
               rwkv_w2, rwkv_a0, rwkv_a2, rwkv_g2, rwkv_k_k, rwkv_k_a, rwkv_r_k, rwkv_lnx_g, rwkv_lnx_b,
               w_rwkv_o, w_ret_o, w_out, norm2_g, w_ffn_gate, w_ffn_up, w_ffn_down)
    for layer in range(norm1_g.shape[0]):
        x = _layer(x, consts, *(w[layer] for w in weights))
    return x
```

```python
import functools

import jax
import jax.numpy as jnp
from jax import lax
from jax.experimental import pallas as pl
from jax.experimental.pallas import tpu as pltpu

F32 = jnp.float32
BF16 = jnp.bfloat16

HEAD = 64
LANES = 128
D_MODEL = 1024
N_HEADS = 8
WIDTH = N_HEADS * HEAD
N_PAIRS = WIDTH // LANES
ATTN_BLOCK = 128
RET_CHUNK = 128
RET_BLOCK = 512
RWKV_CHUNK = 64
RWKV_BLOCK = 256
NORM_EPS = 1e-6
RWKV_LN_EPS = 64e-5
ROPE_BASE = 10000.0
VMEM_LIMIT = 56 * 1024 * 1024

COL_RET = 0
COL_GATES = 2048
COL_RWKV_R = 5120
COL_RWKV_K = 5632
COL_RWKV_V = 6144
COL_ATTN_Q = 6656
COL_RWKV_LORA = 7168
COL_ATTN_K = 7424
COL_ATTN_V = 7552
IN_WIDTH = 7680

NN = (((1,), (0,)), ((), ()))
NT = (((1,), (1,)), ((), ()))
TN = (((0,), (0,)), ((), ()))


def _dot(a, b, dims=NN, precision=None):
    return lax.dot_general(a, b, dims, precision=precision, preferred_element_type=F32)


def _seg_sum(x, bd, split=True):
    cols = []
    for j in range(x.shape[1] // LANES):
        xc = x[:, j * LANES:(j + 1) * LANES]
        hi = xc.astype(BF16)
        s = _dot(hi, bd)
        if split:
            s = s + _dot((xc - hi.astype(F32)).astype(BF16), bd)
        cols.append(s)
    return cols[0] if len(cols) == 1 else jnp.concatenate(cols, axis=1)


def _params(*sem):
    return pltpu.CompilerParams(dimension_semantics=sem, vmem_limit_bytes=VMEM_LIMIT)


def _resident(shape):
    nd = len(shape)
    return pl.BlockSpec(shape, lambda *_: (0,) * nd, pipeline_mode=pl.Buffered(1))


_IN_PROJ_PIECES = ((0, 512, COL_ATTN_Q), (512, 256, COL_ATTN_K), (768, 1536, COL_RWKV_R), (2304, 256, COL_RWKV_LORA),
                   (2560, 1280, 0), (3840, 1280, 1280), (5120, 1280, 2560), (6400, 1280, 3840))


def _rms_matmul_kernel(x_ref, g_ref, w_ref, o_ref):
    x = x_ref[...]
    ms = jnp.mean(x * x, axis=-1, keepdims=True)
    h = (x * lax.rsqrt(ms + NORM_EPS) * g_ref[...]).astype(BF16)
    for src, width, dst in _IN_PROJ_PIECES:
        o_ref[:, dst:dst + width] = _dot(h, w_ref[:, src:src + width]).astype(o_ref.dtype)


def _rms_matmul(x, g, w, tm):
    n, d = x.shape
    cols = w.shape[1]
    return pl.pallas_call(
        _rms_matmul_kernel,
        grid=(n // tm,),
        in_specs=[pl.BlockSpec((tm, d), lambda i: (i, 0)), _resident((1, d)), _resident((d, cols))],
        out_specs=pl.BlockSpec((tm, cols), lambda i: (i, 0)),
        out_shape=jax.ShapeDtypeStruct((n, cols), BF16),
        compiler_params=_params("arbitrary"),
        name="in_proj",
    )(x, g, w)


def _attn_kernel(sink_ref, q_ref, kc_ref, vc_ref, kp_ref, vp_ref, gq_ref, gk_ref, bd_ref, o_ref, *, tq):
    i = pl.program_id(1)
    bd = bd_ref[...]
    q = q_ref[0].astype(F32)
    qn = q * lax.rsqrt(_seg_sum(q * q, bd, split=False) * (1.0 / HEAD) + NORM_EPS) * (gq_ref[...] * HEAD ** -0.5)
    k = jnp.concatenate([kp_ref[0], kc_ref[0]], axis=0).astype(F32)
    kn = k * lax.rsqrt(_seg_sum(k * k, bd, split=False) * (1.0 / HEAD) + NORM_EPS) * gk_ref[...]
    v = jnp.concatenate([vp_ref[0], vc_ref[0]], axis=0).astype(F32)

    lane_k = lax.broadcasted_iota(jnp.int32, k.shape, 1)
    lane_q = lax.broadcasted_iota(jnp.int32, (ATTN_BLOCK, LANES), 1)
    low_q = lane_q < HEAD
    row = lax.broadcasted_iota(jnp.int32, (2 * ATTN_BLOCK, ATTN_BLOCK), 0)
    col = lax.broadcasted_iota(jnp.int32, (2 * ATTN_BLOCK, ATTN_BLOCK), 1)
    from_prev = col > (row & (ATTN_BLOCK - 1))
    row1 = lax.broadcasted_iota(jnp.int32, (2 * ATTN_BLOCK, 1), 0)

    k_dup, v_lo, v_hi = [], [], []
    for e in range(2):
        in_e = (lane_k >= e * HEAD) & (lane_k < (e + 1) * HEAD)
        k_e = jnp.where(in_e, kn, 0.0)
        k_dup.append((k_e + pltpu.roll(k_e, HEAD, 1)).astype(BF16))
        v_e = jnp.where(in_e, v, 0.0)
        lo = v_e if e == 0 else pltpu.roll(v_e, HEAD, 1)
        v_lo.append(lo.astype(BF16))
        v_hi.append(pltpu.roll(lo, HEAD, 1).astype(BF16))

    nb = tq // ATTN_BLOCK
    tiles = [(n, c) for n in range(nb) for c in range(N_PAIRS)]

    def keys(arr, n, c):
        return arr[c // 2][n * ATTN_BLOCK:(n + 2) * ATTN_BLOCK]

    qs = []
    for n, c in tiles:
        qc = qn[n * ATTN_BLOCK:(n + 1) * ATTN_BLOCK, c * LANES:(c + 1) * LANES]
        qs.append(jnp.concatenate([jnp.where(low_q, qc, 0.0), jnp.where(low_q, 0.0, qc)], axis=0).astype(BF16))
    s = [_dot(q_, keys(k_dup, n, c), NT) for q_, (n, c) in zip(qs, tiles)]
    bias = [jnp.where(i * nb + n == 0, -jnp.inf, 0.0) for n, c in tiles]
    s = [jnp.where(from_prev, s_[:, :ATTN_BLOCK] + b_, s_[:, ATTN_BLOCK:]) for s_, b_ in zip(s, bias)]
    sink = [jnp.where(row1 < ATTN_BLOCK, sink_ref[2 * c], sink_ref[2 * c + 1]) for n, c in tiles]
    m = [jnp.maximum(jnp.max(s_, axis=-1, keepdims=True), k_) for s_, k_ in zip(s, sink)]
    p = [jnp.exp(s_ - m_) for s_, m_ in zip(s, m)]
    den = [jnp.sum(p_, axis=-1, keepdims=True) + jnp.exp(k_ - m_) for p_, k_, m_ in zip(p, sink, m)]
    p = [p_.astype(BF16) for p_ in p]
    zero = jnp.zeros_like(p[0])
    pb = [jnp.concatenate([jnp.where(from_prev, p_, zero), jnp.where(from_prev, zero, p_)], axis=1) for p_ in p]
    o = [_dot(p_[:ATTN_BLOCK], keys(v_lo, n, c)) + _dot(p_[ATTN_BLOCK:], keys(v_hi, n, c))
         for p_, (n, c) in zip(pb, tiles)]
    for o_, d_, (n, c) in zip(o, den, tiles):
        inv = 1.0 / d_
        o_ = o_ * jnp.where(low_q, inv[:ATTN_BLOCK], inv[ATTN_BLOCK:])
        o_ref[0, n * ATTN_BLOCK:(n + 1) * ATTN_BLOCK, c * LANES:(c + 1) * LANES] = o_.astype(o_ref.dtype)


def _attention(proj, sinks, gq, gk, bd, tq):
    b, t, _ = proj.shape
    nb = tq // ATTN_BLOCK
    kernel = functools.partial(_attn_kernel, tq=tq)
    return pl.pallas_call(
        kernel,
        grid=(b, t // tq),
        in_specs=[
            pl.BlockSpec(memory_space=pltpu.SMEM),
            pl.BlockSpec((1, tq, WIDTH), lambda bi, i: (bi, i, COL_ATTN_Q // WIDTH)),
            pl.BlockSpec((1, tq, LANES), lambda bi, i: (bi, i, COL_ATTN_K // LANES)),
            pl.BlockSpec((1, tq, LANES), lambda bi, i: (bi, i, COL_ATTN_V // LANES)),
            pl.BlockSpec((1, ATTN_BLOCK, LANES),
                         lambda bi, i: (bi, jnp.maximum(i * nb - 1, 0), COL_ATTN_K // LANES)),
            pl.BlockSpec((1, ATTN_BLOCK, LANES),
                         lambda bi, i: (bi, jnp.maximum(i * nb - 1, 0), COL_ATTN_V // LANES)),
            pl.BlockSpec((1, WIDTH), lambda bi, i: (0, 0)),
            pl.BlockSpec((1, LANES), lambda bi, i: (0, 0)),
            pl.BlockSpec((LANES, LANES), lambda bi, i: (0, 0)),
        ],
        out_specs=pl.BlockSpec((1, tq, WIDTH), lambda bi, i: (bi, i, 0)),
        out_shape=jax.ShapeDtypeStruct((b, t, WIDTH), BF16),
        compiler_params=_params("arbitrary", "arbitrary"),
        name="swa_attention",
    )(sinks, proj, proj, proj, proj, proj, gq, gk, bd)


def _ret_kernel(x_ref, cos_ref, sin_ref, dmat_ref, xi_ref, zeta_ref, cd_ref, bd_ref, o_ref, state_ref, acc_ref):
    @pl.when(pl.program_id(1) == 0)
    def _():
        state_ref[...] = jnp.zeros_like(state_ref)

    c = RET_CHUNK
    nc = x_ref.shape[1] // c
    lane = lax.broadcasted_iota(jnp.int32, (c, LANES), 1)
    first_half = (lane & (HEAD - 1)) < HEAD // 2
    low = lane < HEAD
    rr = lax.broadcasted_iota(jnp.int32, (LANES, LANES), 0)
    cc = lax.broadcasted_iota(jnp.int32, (LANES, LANES), 1)
    same_head = (rr < HEAD) == (cc < HEAD)
    tiles = [(ci, p) for ci in range(nc) for p in range(N_PAIRS)]

    def rotary(col, ci, p):
        rows = slice(ci * c, (ci + 1) * c)
        x = x_ref[0, rows, col + p * LANES:col + (p + 1) * LANES].astype(F32)
        swapped = jnp.where(first_half, pltpu.roll(x, LANES - HEAD // 2, 1), pltpu.roll(x, HEAD // 2, 1))
        sl = slice(p * LANES, (p + 1) * LANES)
        return x * cos_ref[rows, sl] + swapped * sin_ref[rows, sl]

    def split_heads(x):
        zero = jnp.zeros_like(x)
        return jnp.concatenate([jnp.where(low, x, zero), jnp.where(low, zero, x)], axis=0)

    q = [rotary(0, ci, p) for ci, p in tiles]
    k = [rotary(WIDTH, ci, p) * HEAD ** -0.5 for ci, p in tiles]
    v = [x_ref[0, ci * c:(ci + 1) * c, 2 * WIDTH + p * LANES:2 * WIDTH + (p + 1) * LANES] for ci, p in tiles]
    s = [_dot(split_heads(q_.astype(BF16)), k_.astype(BF16), NT) * dmat_ref[p]
         for q_, k_, (ci, p) in zip(q, k, tiles)]
    inner = [_dot(jnp.concatenate([s_[:c], s_[c:]], axis=1).astype(BF16), split_heads(v_)) for s_, v_ in zip(s, v)]
    kv = [jnp.where(same_head, _dot((k_ * zeta_ref[:, p * LANES:(p + 1) * LANES]).astype(BF16), v_, TN), 0.0)
          for k_, v_, (ci, p) in zip(k, v, tiles)]
    entering = []
    state = [state_ref[p] for p in range(N_PAIRS)]
    for ci in range(nc):
        for p in range(N_PAIRS):
            entering.append(state[p])
            state[p] = state[p] * cd_ref[p] + kv[ci * N_PAIRS + p]
    for p in range(N_PAIRS):
        state_ref[p] = state[p]
    for n, (ci, p) in enumerate(tiles):
        cross = _dot((q[n] * xi_ref[:, p * LANES:(p + 1) * LANES]).astype(BF16), entering[n].astype(BF16))
        acc_ref[ci * c:(ci + 1) * c, p * LANES:(p + 1) * LANES] = inner[n] + cross
    o = acc_ref[...]
    o = o * lax.rsqrt(_seg_sum(o * o, bd_ref[...], split=False) * (1.0 / HEAD) + NORM_EPS)
    g = x_ref[0, :, 3 * WIDTH:4 * WIDTH].astype(F32)
    o_ref[0] = (o * (g * jax.nn.sigmoid(g))).astype(o_ref.dtype)


def _retention(proj, tables, bd):
    b, t, _ = proj.shape
    c = RET_CHUNK
    tb = RET_BLOCK
    cos, sin, dmat, xi, zeta, cd = tables
    return pl.pallas_call(
        _ret_kernel,
        grid=(b, t // tb),
        in_specs=[
            pl.BlockSpec((1, tb, 4 * WIDTH), lambda bi, n: (bi, n, COL_RET // (4 * WIDTH))),
            pl.BlockSpec((tb, WIDTH), lambda bi, n: (n, 0)),
            pl.BlockSpec((tb, WIDTH), lambda bi, n: (n, 0)),
            pl.BlockSpec((N_PAIRS, 2 * c, c), lambda bi, n: (0, 0, 0)),
            pl.BlockSpec((c, WIDTH), lambda bi, n: (0, 0)),
            pl.BlockSpec((c, WIDTH), lambda bi, n: (0, 0)),
            pl.BlockSpec((N_PAIRS, LANES, LANES), lambda bi, n: (0, 0, 0)),
            pl.BlockSpec((LANES, LANES), lambda bi, n: (0, 0)),
        ],
        out_specs=pl.BlockSpec((1, tb, WIDTH), lambda bi, n: (bi, n, 0)),
        out_shape=jax.ShapeDtypeStruct((b, t, WIDTH), BF16),
        scratch_shapes=[pltpu.VMEM((N_PAIRS, LANES, LANES), F32), pltpu.VMEM((tb, WIDTH), F32)],
        compiler_params=_params("arbitrary", "arbitrary"),
        name="retention",
    )(proj, cos, sin, dmat, xi, zeta, cd, bd)


def _retention_tables(t):
    c = RET_CHUNK
    half = HEAD // 2
    inv_freq = 1.0 / (ROPE_BASE ** (jnp.arange(half, dtype=F32) * 2.0 / HEAD))
    ang = jnp.arange(t, dtype=F32)[:, None] * inv_freq[None, :]
    cos = jnp.tile(jnp.concatenate([jnp.cos(ang), jnp.cos(ang)], axis=1), (1, N_HEADS))
    sin = jnp.tile(jnp.concatenate([-jnp.sin(ang), jnp.sin(ang)], axis=1), (1, N_HEADS))
    log_gamma = jnp.log1p(-jnp.power(2.0, -5.0 - jnp.arange(N_HEADS, dtype=F32)))
    idx = jnp.arange(c, dtype=F32)
    diff = idx[:, None] - idx[None, :]
    dmat = jnp.where(diff >= 0, jnp.exp(log_gamma[:, None, None] * jnp.maximum(diff, 0.0)), 0.0)
    xi = jnp.exp(log_gamma[:, None] * (idx[None, :] + 1.0))
    zeta = jnp.exp(log_gamma[:, None] * (c - 1.0 - idx[None, :]))
    chunk_decay = jnp.exp(log_gamma * c)
    xi = jnp.repeat(xi.T, HEAD, axis=1)
    zeta = jnp.repeat(zeta.T, HEAD, axis=1)
    cd = jnp.broadcast_to(jnp.repeat(chunk_decay, HEAD).reshape(N_PAIRS, LANES, 1), (N_PAIRS, LANES, LANES))
    dmat = dmat.reshape(N_PAIRS, 2 * c, c)
    return cos, sin, dmat, xi, zeta, cd


def _bdot(a, b, dims=NN):
    return _dot(a.astype(BF16), b.astype(BF16), dims)


def _rwkv_kernel(r_ref, k_ref, v_ref, lo_ref, mu_ref, w0_ref, a0_ref, kk_ref, ka_ref, rk_ref,
                 lng_ref, lnb_ref, wl_ref, tri_ref, bd_ref, o_ref,
                 carry_ref, state_ref, y_ref):
    L = RWKV_CHUNK
    tb = r_ref.shape[1]
    nc = tb // L

    @pl.when(pl.program_id(1) == 0)
    def _():
        carry_ref[...] = jnp.zeros_like(carry_ref)
        state_ref[...] = jnp.zeros_like(state_ref)

    bd = bd_ref[...]

    def mix(ref, slot, width, mu):
        h = ref[0].astype(F32)
        row = lax.broadcasted_iota(jnp.int32, h.shape, 0)
        prev = jnp.where(row == 0, carry_ref[slot:slot + 1, :width], pltpu.roll(h, 1, 0))
        carry_ref[slot:slot + 1, :width] = h[tb - 1:tb]
        return h + mu * (prev - h)

    r = mix(r_ref, 0, WIDTH, mu_ref[:, 0:WIDTH])
    k = mix(k_ref, 1, WIDTH, mu_ref[:, WIDTH:2 * WIDTH])
    v = mix(v_ref, 2, WIDTH, mu_ref[:, 2 * WIDTH:3 * WIDTH])
    lo = mix(lo_ref, 3, 2 * LANES, mu_ref[:, 3 * WIDTH:3 * WIDTH + 2 * LANES])

    lane_lo = lax.broadcasted_iota(jnp.int32, lo.shape, 1)
    act = jnp.where(lane_lo < HEAD, jnp.tanh(lo), jnp.where(lane_lo < 2 * HEAD, lo, jax.nn.sigmoid(lo)))
    lora = _dot(act.astype(BF16), wl_ref[...])
    wx = -(w0_ref[...] + lora[:, 0:WIDTH])
    softplus = jnp.maximum(wx, 0.0) + jnp.log(1.0 + jnp.exp(-jnp.abs(wx)))
    lw = -jnp.exp(-softplus - 0.5)
    a = jax.nn.sigmoid(a0_ref[...] + lora[:, WIDTH:2 * WIDTH])
    g = lora[:, 2 * WIDTH:3 * WIDTH]
    kkr = k * kk_ref[...]
    kk = kkr / jnp.maximum(jnp.sqrt(_seg_sum(kkr * kkr, bd, split=False)), 1e-12)
    k2 = k * (1.0 + (a - 1.0) * ka_ref[...])

    lw_hi = lw.astype(BF16)
    lw_lo = (lw - lw_hi.astype(F32)).astype(BF16)
    cum = _dot(tri_ref[...], lw_hi) + _dot(tri_ref[...], lw_lo)
    e_pos = jnp.exp(cum)
    e_neg = jnp.exp(-cum)
    r_t = r * e_pos
    a_t = -kk * jnp.exp(cum - lw)
    b_t = kk * a * e_neg
    k_t = k2 * e_neg

    lane = lax.broadcasted_iota(jnp.int32, (L, LANES), 1)
    trow = lax.broadcasted_iota(jnp.int32, (L, LANES), 0)
    low = lane < HEAD
    tcol = lane & (HEAD - 1)
    strict = tcol < trow
    incl = tcol <= trow
    eye = (tcol == trow).astype(F32)
    rr = lax.broadcasted_iota(jnp.int32, (LANES, LANES), 0)
    cc = lax.broadcasted_iota(jnp.int32, (LANES, LANES), 1)
    same_head = (rr < HEAD) == (cc < HEAD)
    diag = rr == cc

    def bdiag(z):
        z = z.astype(BF16)
        zero = jnp.zeros_like(z)
        return jnp.concatenate([jnp.where(low, z, zero), jnp.where(low, zero, z)], axis=0)

    def bmm(x, z):
        return _dot(x.astype(BF16), bdiag(z))

    tiles = [(c, p) for c in range(nc) for p in range(N_PAIRS)]

    def tile(arr, c, p):
        return arr[c * L:(c + 1) * L, p * LANES:(p + 1) * LANES]

    at_ = [tile(a_t, c, p) for c, p in tiles]
    rt_ = [tile(r_t, c, p) for c, p in tiles]
    bt_ = [tile(b_t, c, p) for c, p in tiles]
    kt_ = [tile(k_t, c, p) for c, p in tiles]
    vp_ = [tile(v, c, p) for c, p in tiles]
    def stack(top, bottom):
        return jnp.concatenate([top, bottom], axis=0).astype(BF16)

    tt = [_dot(stack(a_, r_), jnp.concatenate([bdiag(b_), bdiag(k_)], axis=0), NT)
          for a_, r_, b_, k_ in zip(at_, rt_, bt_, kt_)]
    t_ab = [jnp.where(strict, m[:L, :LANES], 0.0) for m in tt]
    t_rb = [jnp.where(incl, m[L:, :LANES], 0.0) for m in tt]
    t_ak = [jnp.where(strict, m[:L, LANES:], 0.0) for m in tt]
    t_rk = [jnp.where(incl, m[L:, LANES:], 0.0) for m in tt]
    inv = [eye + t for t in t_ab]
    pw = [bmm(t, t) for t in t_ab]
    for _ in range(4):
        st = [_dot(stack(w, i), bdiag(w)) for w, i in zip(pw, inv)]
        inv = [i + s[L:] for i, s in zip(inv, st)]
        pw = [s[:L] for s in st]
    inv = [i + bmm(i, w) for i, w in zip(inv, pw)]
    tvk = [_dot(stack(ta, tk), bdiag(v_)) for ta, tk, v_ in zip(t_ak, t_rk, vp_)]
    lhs = [stack(i, bmm(t, i)) for i, t in zip(inv, t_rb)]
    ra = [_dot(x, bdiag(a_)) for x, a_ in zip(lhs, at_)]
    ru = [_dot(x, bdiag(s[:L])) for x, s in zip(lhs, tvk)]
    qe = [r_ + s[L:] for r_, s in zip(rt_, ra)]
    yl = [s[L:] + s2[L:] for s, s2 in zip(ru, tvk)]
    e_end = [e_pos[(c + 1) * L - 1:(c + 1) * L, p * LANES:(p + 1) * LANES] for c, p in tiles]
    bh = [b_ * e for b_, e in zip(bt_, e_end)]
    kh = [k_ * e for k_, e in zip(kt_, e_end)]
    au = [_bdot(jnp.concatenate([s[:L], s2[:L]], axis=1), b_, TN) for s, s2, b_ in zip(ra, ru, bh)]
    pm = [jnp.where(same_head, s[:LANES], 0.0) + jnp.where(diag, e, 0.0) for s, e in zip(au, e_end)]
    gm = [jnp.where(same_head, s[LANES:] + _bdot(v_, k_, TN), 0.0) for s, v_, k_ in zip(au, vp_, kh)]

    state = [state_ref[p] for p in range(N_PAIRS)]
    for c in range(nc):
        for p in range(N_PAIRS):
            n = c * N_PAIRS + p
            y_ref[c * L:(c + 1) * L, p * LANES:(p + 1) * LANES] = _bdot(qe[n], state[p], NT) + yl[n]
        state = [_bdot(state[p], pm[c * N_PAIRS + p]) + gm[c * N_PAIRS + p] for p in range(N_PAIRS)]
    for p in range(N_PAIRS):
        state_ref[p] = state[p]

    y = y_ref[...]
    mean = _seg_sum(y, bd) * (1.0 / HEAD)
    d = y - mean
    var = _seg_sum(d * d, bd, split=False) * (1.0 / HEAD)
    yn = d * lax.rsqrt(var + RWKV_LN_EPS) * lng_ref[...] + lnb_ref[...]
    bonus = _seg_sum(r * k2 * rk_ref[...], bd) * v
    o_ref[0] = ((yn + bonus) * g).astype(o_ref.dtype)


def _rwkv(proj, prm, tri, bd):
    b, t, _ = proj.shape
    tb = RWKV_BLOCK
    nc = tb // RWKV_CHUNK
    row = lambda w: pl.BlockSpec((1, w), lambda bi, n: (0, 0))
    return pl.pallas_call(
        _rwkv_kernel,
        grid=(b, t // tb),
        in_specs=[
            pl.BlockSpec((1, tb, WIDTH), lambda bi, n: (bi, n, COL_RWKV_R // WIDTH)),
            pl.BlockSpec((1, tb, WIDTH), lambda bi, n: (bi, n, COL_RWKV_K // WIDTH)),
            pl.BlockSpec((1, tb, WIDTH), lambda bi, n: (bi, n, COL_RWKV_V // WIDTH)),
            pl.BlockSpec((1, tb, 2 * LANES), lambda bi, n: (bi, n, COL_RWKV_LORA // (2 * LANES))),
            row(3 * WIDTH + 2 * LANES), row(WIDTH), row(WIDTH), row(WIDTH), row(WIDTH), row(WIDTH),
            row(WIDTH), row(WIDTH),
            pl.BlockSpec((2 * LANES, 3 * WIDTH), lambda bi, n: (0, 0)),
            pl.BlockSpec((tb, tb), lambda bi, n: (0, 0)),
            pl.BlockSpec((LANES, LANES), lambda bi, n: (0, 0)),
        ],
        out_specs=pl.BlockSpec((1, tb, WIDTH), lambda bi, n: (bi, n, 0)),
        out_shape=jax.ShapeDtypeStruct((b, t, WIDTH), BF16),
        scratch_shapes=[
            pltpu.VMEM((8, WIDTH), F32),
            pltpu.VMEM((N_PAIRS, LANES, LANES), F32),
            pltpu.VMEM((tb, WIDTH), F32),
        ],
        compiler_params=_params("arbitrary", "arbitrary"),
        name="rwkv7",
    )(proj, proj, proj, proj, *prm, tri, bd)


def _merge_kernel(oa_ref, ob_ref, oc_ref, g0_ref, g1_ref, g2_ref, x_ref, wa_ref, wb_ref, wc_ref, wo_ref, o_ref):
    def branch(o, w, g):
        return jax.nn.sigmoid(g[...].astype(F32)) * _dot(o[...], w[...])

    mixed = branch(oa_ref, wa_ref, g0_ref) + branch(ob_ref, wb_ref, g1_ref) + branch(oc_ref, wc_ref, g2_ref)
    o_ref[...] = x_ref[...] + _dot(mixed.astype(BF16), wo_ref[...])


def _merge(x, proj, oa, ob, oc, wa, wb, wc, wo, tm):
    n, d = x.shape
    branch = pl.BlockSpec((tm, WIDTH), lambda i: (i, 0))
    gate = lambda j: pl.BlockSpec((tm, d), lambda i: (i, COL_GATES // d + j))
    return pl.pallas_call(
        _merge_kernel,
        grid=(n // tm,),
        in_specs=[branch, branch, branch, gate(0), gate(1), gate(2),
                  pl.BlockSpec((tm, d), lambda i: (i, 0)),
                  _resident((WIDTH, d)), _resident((WIDTH, d)), _resident((WIDTH, d)), _resident((d, d))],
        out_specs=pl.BlockSpec((tm, d), lambda i: (i, 0)),
        out_shape=jax.ShapeDtypeStruct((n, d), F32),
        compiler_params=_params("arbitrary"),
        name="merge_out_proj",
    )(oa, ob, oc, proj, proj, proj, x, wa, wb, wc, wo)


def _ffn_kernel(x_ref, g_ref, wg_ref, wu_ref, wd_ref, o_ref):
    x = x_ref[...]
    ms = jnp.mean(x * x, axis=-1, keepdims=True)
    h = (x * lax.rsqrt(ms + NORM_EPS) * g_ref[...]).astype(BF16)
    gate = _dot(h, wg_ref[...])
    up = _dot(h, wu_ref[...])
    act = (gate * jax.nn.sigmoid(gate) * up).astype(BF16)
    o_ref[...] = x + _dot(act, wd_ref[...])


def _ffn(x, g, wg, wu, wd, tm):
    n, d = x.shape
    f = wg.shape[1]
    return pl.pallas_call(
        _ffn_kernel,
        grid=(n // tm,),
        in_specs=[pl.BlockSpec((tm, d), lambda i: (i, 0)), _resident((1, d)),
                  _resident((d, f)), _resident((d, f)), _resident((f, d))],
        out_specs=pl.BlockSpec((tm, d), lambda i: (i, 0)),
        out_shape=jax.ShapeDtypeStruct((n, d), F32),
        compiler_params=_params("arbitrary"),
        name="swiglu_ffn",
    )(x, g, wg, wu, wd)


def _lora_weight(w2, a2, g2):
    w = jnp.zeros((2 * LANES, 3 * WIDTH), F32)
    w = w.at[0:64, 0:WIDTH].set(w2)
    w = w.at[64:128, WIDTH:2 * WIDTH].set(a2)
    w = w.at[128:256, 2 * WIDTH:3 * WIDTH].set(g2)
    return w.astype(BF16)


def _layer(x, consts, norm1_g, w_in, attn_q_norm_g, attn_k_norm_g, attn_sinks, w_attn_o,
           rwkv_shift_mu, rwkv_w0, rwkv_w2, rwkv_a0, rwkv_a2, rwkv_g2, rwkv_k_k, rwkv_k_a,
           rwkv_r_k, rwkv_lnx_g, rwkv_lnx_b, w_rwkv_o, w_ret_o, w_out,
           norm2_g, w_ffn_gate, w_ffn_up, w_ffn_down):
    b, t, d = x.shape
    n = b * t
    bd, tri, ret_tables = consts
    x2 = x.reshape(n, d)
    proj = _rms_matmul(x2, norm1_g.reshape(1, d), w_in.astype(BF16), tm=512)
    proj3 = proj.reshape(b, t, IN_WIDTH)

    o_a = _attention(proj3, attn_sinks, jnp.tile(attn_q_norm_g, N_HEADS).reshape(1, WIDTH),
                     jnp.tile(attn_k_norm_g, 2).reshape(1, LANES), bd, tq=512)

    row = lambda a: a.reshape(1, WIDTH)
    prm = (rwkv_shift_mu.reshape(1, -1), row(rwkv_w0), row(rwkv_a0), row(rwkv_k_k), row(rwkv_k_a), row(rwkv_r_k),
           row(rwkv_lnx_g), row(rwkv_lnx_b), _lora_weight(rwkv_w2, rwkv_a2, rwkv_g2))
    o_b = _rwkv(proj3, prm, tri, bd)

    o_c = _retention(proj3, ret_tables, bd)

    x2 = _merge(x2, proj, o_a.reshape(n, WIDTH), o_b.reshape(n, WIDTH), o_c.reshape(n, WIDTH),
                w_attn_o.astype(BF16), w_rwkv_o.astype(BF16), w_ret_o.astype(BF16), w_out.astype(BF16), tm=512)
    x2 = _ffn(x2, norm2_g.reshape(1, d), w_ffn_gate.astype(BF16), w_ffn_up.astype(BF16),
              w_ffn_down.astype(BF16), tm=512)
    return x2.reshape(b, t, d)


def _constants(t):
    head_id = jnp.arange(LANES) // HEAD
    bd = (head_id[:, None] == head_id[None, :]).astype(BF16)
    idx = jnp.arange(RWKV_BLOCK)
    chunk = idx // RWKV_CHUNK
    tri = ((idx[None, :] <= idx[:, None]) & (chunk[None, :] == chunk[:, None])).astype(BF16)
    return bd, tri, _retention_tables(t)


def kernel(x, norm1_g, w_in, attn_q_norm_g, attn_k_norm_g, attn_sinks, w_attn_o, rwkv_shift_mu, rwkv_w0, rwkv_w2, rwkv_a0, rwkv_a2, rwkv_g2, rwkv_k_k, rwkv_k_a, rwkv_r_k, rwkv_lnx_g, rwkv_lnx_b, w_rwkv_o, w_ret_o, w_out, norm2_g, w_ffn_gate, w_ffn_up, w_ffn_down):
    consts = _constants(x.shape[1])
    weights = (norm1_g, w_in, attn_q_norm_g, attn_k_norm_g, attn_sinks, w_attn_o, rwkv_shift_mu, rwkv_w0,
               rwkv_w2, rwkv_a0, rwkv_a2, rwkv_g2, rwkv_k_k, rwkv_k_a, rwkv_r_k, rwkv_lnx_g, rwkv_lnx_b,
               w_rwkv_o, w_ret_o, w_out, norm2_g, w_ffn_gate, w_ffn_up, w_ffn_down)
    for layer in range(norm1_g.shape[0]):
        x = _layer(x, consts, *(w[layer] for w in weights))
    return x
```

```python
import functools

import jax
import jax.numpy as jnp
from jax import lax
from jax.experimental import pallas as pl
from jax.experimental.pallas import tpu as pltpu

F32 = jnp.float32
BF16 = jnp.bfloat16

HEAD = 64
LANES = 128
D_MODEL = 1024
N_HEADS = 8
WIDTH = N_HEADS * HEAD
N_PAIRS = WIDTH // LANES
ATTN_BLOCK = 128
RET_CHUNK = 128
RET_BLOCK = 512
RWKV_CHUNK = 64
RWKV_BLOCK = 512
NORM_EPS = 1e-6
RWKV_LN_EPS = 64e-5
ROPE_BASE = 10000.0
VMEM_LIMIT = 56 * 1024 * 1024

COL_RET = 0
COL_GATES = 2048
COL_ATTN_Q = 5120
COL_ATTN_K = 5632
COL_ATTN_V = 5760
MAIN_WIDTH = 5888
SRC_ATTN = 0
SRC_RWKV = 768
SRC_RET = 2560
RWKV_COLS = 1792
OPS_A, OPS_R, OPS_B, OPS_K, OPS_BH, OPS_KH, OPS_V, OPS_G, OPS_BONUS = range(9)
OPS_WIDTH = 9 * WIDTH

NN = (((1,), (0,)), ((), ()))
NT = (((1,), (1,)), ((), ()))
TN = (((0,), (0,)), ((), ()))


def _dot(a, b, dims=NN, precision=None):
    return lax.dot_general(a, b, dims, precision=precision, preferred_element_type=F32)


def _seg_sum(x, bd, split=True):
    cols = []
    for j in range(x.shape[1] // LANES):
        xc = x[:, j * LANES:(j + 1) * LANES]
        hi = xc.astype(BF16)
        s = _dot(hi, bd)
        if split:
            s = s + _dot((xc - hi.astype(F32)).astype(BF16), bd)
        cols.append(s)
    return cols[0] if len(cols) == 1 else jnp.concatenate(cols, axis=1)


def _params(*sem):
    return pltpu.CompilerParams(dimension_semantics=sem, vmem_limit_bytes=VMEM_LIMIT)


def _resident(shape):
    nd = len(shape)
    return pl.BlockSpec(shape, lambda *_: (0,) * nd, pipeline_mode=pl.Buffered(1))


IN_PROJ_ROWS = 512
PREP_ROWS = 128
PIECE = 256


def _main_pieces():
    pieces = [(SRC_RET + d, d) for d in range(0, COL_ATTN_Q, PIECE)]
    pieces += [(SRC_ATTN + d, COL_ATTN_Q + d) for d in range(0, MAIN_WIDTH - COL_ATTN_Q, PIECE)]
    return pieces


def _in_proj_kernel(x_ref, g_ref, w_ref, mu_ref, w0_ref, a0_ref, kk_ref, ka_ref, rk_ref, wl_ref, tri_ref, bd_ref,
                    main_ref, ops_ref, eend_ref, carry_ref, *, blocks_per_row):
    tm = x_ref.shape[0]
    L = RWKV_CHUNK

    @pl.when(pl.program_id(0) == 0)
    def _():
        carry_ref[...] = jnp.zeros_like(carry_ref)

    x = x_ref[...]
    ms = jnp.mean(x * x, axis=-1, keepdims=True)
    h = (x * lax.rsqrt(ms + NORM_EPS) * g_ref[...]).astype(BF16)
    hr = _dot(h, w_ref[:, SRC_RWKV:SRC_RWKV + RWKV_COLS])
    bd = bd_ref[...]
    tri = tri_ref[...]
    prev_row = jnp.where(pl.program_id(0) % blocks_per_row == 0, 0.0, carry_ref[0:1, :])
    carry_ref[0:1, :] = hr[tm - 1:tm]

    def put(rows, slot, val):
        ops_ref[rows, slot * WIDTH:(slot + 1) * WIDTH] = val.astype(BF16)

    pieces = iter(_main_pieces())

    def emit_pieces(count):
        for _ in range(count):
            piece = next(pieces, None)
            if piece is not None:
                src, dst = piece
                main_ref[:, dst:dst + PIECE] = _dot(h, w_ref[:, src:src + PIECE]).astype(main_ref.dtype)

    groups = tm // PREP_ROWS
    per_stage = -(-len(_main_pieces()) // (2 * groups))
    for gi in range(groups):
        r0 = gi * PREP_ROWS
        rows = slice(r0, r0 + PREP_ROWS)
        hb = hr[rows]
        row = lax.broadcasted_iota(jnp.int32, hb.shape, 0)
        prev = jnp.where(row == 0, prev_row, pltpu.roll(hb, 1, 0))
        prev_row = hb[PREP_ROWS - 1:PREP_ROWS]
        z = hb + mu_ref[...] * (prev - hb)
        r, k, v, lo = z[:, 0:WIDTH], z[:, WIDTH:2 * WIDTH], z[:, 2 * WIDTH:3 * WIDTH], z[:, 3 * WIDTH:]
        lane_lo = lax.broadcasted_iota(jnp.int32, lo.shape, 1)
        act = jnp.where(lane_lo < HEAD, jnp.tanh(lo), jnp.where(lane_lo < 2 * HEAD, lo, jax.nn.sigmoid(lo)))
        kkr = k * kk_ref[...]
        lora = _dot(act.astype(BF16), wl_ref[...])
        kk_norm = _seg_sum(kkr * kkr, bd, split=False)
        emit_pieces(per_stage)
        wx = -(w0_ref[...] + lora[:, 0:WIDTH])
        softplus = jnp.maximum(wx, 0.0) + jnp.log(1.0 + jnp.exp(-jnp.abs(wx)))
        lw = -jnp.exp(-softplus - 0.5)
        a = jax.nn.sigmoid(a0_ref[...] + lora[:, WIDTH:2 * WIDTH])
        kk = kkr / jnp.maximum(jnp.sqrt(kk_norm), 1e-12)
        k2 = k * (1.0 + (a - 1.0) * ka_ref[...])
        lw_hi = lw.astype(BF16)
        lw_lo = (lw - lw_hi.astype(F32)).astype(BF16)
        chunks = [slice(c0, c0 + L) for c0 in range(0, PREP_ROWS, L)]
        cums = [_dot(tri, lw_hi[cs]) + _dot(tri, lw_lo[cs]) for cs in chunks]
        bonus = _seg_sum(r * k2 * rk_ref[...], bd) * v
        emit_pieces(per_stage)
        put(rows, OPS_V, v)
        put(rows, OPS_G, lora[:, 2 * WIDTH:3 * WIDTH])
        put(rows, OPS_BONUS, bonus)
        for cs, cum in zip(chunks, cums):
            crow = slice(r0 + cs.start, r0 + cs.stop)
            e_pos = jnp.exp(cum)
            e_neg = jnp.exp(-cum)
            e_end = e_pos[L - 1:L]
            b_t = kk[cs] * a[cs] * e_neg
            k_t = k2[cs] * e_neg
            put(crow, OPS_A, -kk[cs] * jnp.exp(cum - lw[cs]))
            put(crow, OPS_R, r[cs] * e_pos)
            put(crow, OPS_B, b_t)
            put(crow, OPS_K, k_t)
            put(crow, OPS_BH, b_t * e_end)
            put(crow, OPS_KH, k_t * e_end)
            eend_ref[crow.start // L:crow.start // L + 1, :] = e_end
    emit_pieces(len(_main_pieces()))


def _in_proj(x, g, w, prm, tri, bd, tm, blocks_per_row):
    n, d = x.shape
    row = lambda width: _resident((1, width))
    return pl.pallas_call(
        functools.partial(_in_proj_kernel, blocks_per_row=blocks_per_row),
        grid=(n // tm,),
        in_specs=[pl.BlockSpec((tm, d), lambda i: (i, 0)), row(d), _resident(w.shape),
                  row(RWKV_COLS), row(WIDTH), row(WIDTH), row(WIDTH), row(WIDTH), row(WIDTH),
                  _resident((2 * LANES, 3 * WIDTH)), _resident((RWKV_CHUNK, RWKV_CHUNK)), _resident((LANES, LANES))],
        out_specs=[pl.BlockSpec((tm, MAIN_WIDTH), lambda i: (i, 0)),
                   pl.BlockSpec((tm, OPS_WIDTH), lambda i: (i, 0)),
                   pl.BlockSpec((tm // RWKV_CHUNK, WIDTH), lambda i: (i, 0))],
        out_shape=[jax.ShapeDtypeStruct((n, MAIN_WIDTH), BF16),
                   jax.ShapeDtypeStruct((n, OPS_WIDTH), BF16),
                   jax.ShapeDtypeStruct((n // RWKV_CHUNK, WIDTH), F32)],
        scratch_shapes=[pltpu.VMEM((8, RWKV_COLS), F32)],
        compiler_params=_params("arbitrary"),
        name="in_proj",
    )(x, g, w, *prm, tri, bd)


def _attn_kernel(sink_ref, q_ref, kc_ref, vc_ref, kp_ref, vp_ref, gq_ref, gk_ref, bd_ref, o_ref, *, tq):
    i = pl.program_id(1)
    bd = bd_ref[...]
    q = q_ref[0].astype(F32)
    qn = q * lax.rsqrt(_seg_sum(q * q, bd, split=False) * (1.0 / HEAD) + NORM_EPS) * (gq_ref[...] * HEAD ** -0.5)
    k = jnp.concatenate([kp_ref[0], kc_ref[0]], axis=0).astype(F32)
    kn = k * lax.rsqrt(_seg_sum(k * k, bd, split=False) * (1.0 / HEAD) + NORM_EPS) * gk_ref[...]
    v = jnp.concatenate([vp_ref[0], vc_ref[0]], axis=0).astype(F32)

    lane_k = lax.broadcasted_iota(jnp.int32, k.shape, 1)
    lane_q = lax.broadcasted_iota(jnp.int32, (ATTN_BLOCK, LANES), 1)
    low_q = lane_q < HEAD
    row = lax.broadcasted_iota(jnp.int32, (2 * ATTN_BLOCK, ATTN_BLOCK), 0)
    col = lax.broadcasted_iota(jnp.int32, (2 * ATTN_BLOCK, ATTN_BLOCK), 1)
    from_prev = col > (row & (ATTN_BLOCK - 1))
    row1 = lax.broadcasted_iota(jnp.int32, (2 * ATTN_BLOCK, 1), 0)

    k_dup, v_lo, v_hi = [], [], []
    for e in range(2):
        in_e = (lane_k >= e * HEAD) & (lane_k < (e + 1) * HEAD)
        k_e = jnp.where(in_e, kn, 0.0)
        k_dup.append((k_e + pltpu.roll(k_e, HEAD, 1)).astype(BF16))
        v_e = jnp.where(in_e, v, 0.0)
        lo = v_e if e == 0 else pltpu.roll(v_e, HEAD, 1)
        v_lo.append(lo.astype(BF16))
        v_hi.append(pltpu.roll(lo, HEAD, 1).astype(BF16))

    nb = tq // ATTN_BLOCK
    tiles = [(n, c) for n in range(nb) for c in range(N_PAIRS)]

    def keys(arr, n, c):
        return arr[c // 2][n * ATTN_BLOCK:(n + 2) * ATTN_BLOCK]

    qs = []
    for n, c in tiles:
        qc = qn[n * ATTN_BLOCK:(n + 1) * ATTN_BLOCK, c * LANES:(c + 1) * LANES]
        qs.append(jnp.concatenate([jnp.where(low_q, qc, 0.0), jnp.where(low_q, 0.0, qc)], axis=0).astype(BF16))
    s = [_dot(q_, keys(k_dup, n, c), NT) for q_, (n, c) in zip(qs, tiles)]
    bias = [jnp.where(i * nb + n == 0, -jnp.inf, 0.0) for n, c in tiles]
    s = [jnp.where(from_prev, s_[:, :ATTN_BLOCK] + b_, s_[:, ATTN_BLOCK:]) for s_, b_ in zip(s, bias)]
    sink = [jnp.where(row1 < ATTN_BLOCK, sink_ref[2 * c], sink_ref[2 * c + 1]) for n, c in tiles]
    m = [jnp.maximum(jnp.max(s_, axis=-1, keepdims=True), k_) for s_, k_ in zip(s, sink)]
    p = [jnp.exp(s_ - m_) for s_, m_ in zip(s, m)]
    den = [jnp.sum(p_, axis=-1, keepdims=True) + jnp.exp(k_ - m_) for p_, k_, m_ in zip(p, sink, m)]
    p = [p_.astype(BF16) for p_ in p]
    zero = jnp.zeros_like(p[0])
    pb = [jnp.concatenate([jnp.where(from_prev, p_, zero), jnp.where(from_prev, zero, p_)], axis=1) for p_ in p]
    o = [_dot(p_[:ATTN_BLOCK], keys(v_lo, n, c)) + _dot(p_[ATTN_BLOCK:], keys(v_hi, n, c))
         for p_, (n, c) in zip(pb, tiles)]
    for o_, d_, (n, c) in zip(o, den, tiles):
        inv = 1.0 / d_
        o_ = o_ * jnp.where(low_q, inv[:ATTN_BLOCK], inv[ATTN_BLOCK:])
        o_ref[0, n * ATTN_BLOCK:(n + 1) * ATTN_BLOCK, c * LANES:(c + 1) * LANES] = o_.astype(o_ref.dtype)


def _attention(proj, sinks, gq, gk, bd, tq):
    b, t, _ = proj.shape
    nb = tq // ATTN_BLOCK
    kernel = functools.partial(_attn_kernel, tq=tq)
    return pl.pallas_call(
        kernel,
        grid=(b, t // tq),
        in_specs=[
            pl.BlockSpec(memory_space=pltpu.SMEM),
            pl.BlockSpec((1, tq, WIDTH), lambda bi, i: (bi, i, COL_ATTN_Q // WIDTH)),
            pl.BlockSpec((1, tq, LANES), lambda bi, i: (bi, i, COL_ATTN_K // LANES)),
            pl.BlockSpec((1, tq, LANES), lambda bi, i: (bi, i, COL_ATTN_V // LANES)),
            pl.BlockSpec((1, ATTN_BLOCK, LANES),
                         lambda bi, i: (bi, jnp.maximum(i * nb - 1, 0), COL_ATTN_K // LANES)),
            pl.BlockSpec((1, ATTN_BLOCK, LANES),
                         lambda bi, i: (bi, jnp.maximum(i * nb - 1, 0), COL_ATTN_V // LANES)),
            pl.BlockSpec((1, WIDTH), lambda bi, i: (0, 0)),
            pl.BlockSpec((1, LANES), lambda bi, i: (0, 0)),
            pl.BlockSpec((LANES, LANES), lambda bi, i: (0, 0)),
        ],
        out_specs=pl.BlockSpec((1, tq, WIDTH), lambda bi, i: (bi, i, 0)),
        out_shape=jax.ShapeDtypeStruct((b, t, WIDTH), BF16),
        compiler_params=_params("arbitrary", "arbitrary"),
        name="swa_attention",
    )(sinks, proj, proj, proj, proj, proj, gq, gk, bd)


def _ret_kernel(x_ref, cos_ref, sin_ref, dmat_ref, xi_ref, zeta_ref, cd_ref, bd_ref, o_ref, state_ref, acc_ref):
    @pl.when(pl.program_id(1) == 0)
    def _():
        state_ref[...] = jnp.zeros_like(state_ref)

    c = RET_CHUNK
    nc = x_ref.shape[1] // c
    lane = lax.broadcasted_iota(jnp.int32, (c, LANES), 1)
    first_half = (lane & (HEAD - 1)) < HEAD // 2
    low = lane < HEAD
    rr = lax.broadcasted_iota(jnp.int32, (LANES, LANES), 0)
    cc = lax.broadcasted_iota(jnp.int32, (LANES, LANES), 1)
    same_head = (rr < HEAD) == (cc < HEAD)
    tiles = [(ci, p) for ci in range(nc) for p in range(N_PAIRS)]

    def rotary(col, ci, p):
        rows = slice(ci * c, (ci + 1) * c)
        x = x_ref[0, rows, col + p * LANES:col + (p + 1) * LANES].astype(F32)
        swapped = jnp.where(first_half, pltpu.roll(x, LANES - HEAD // 2, 1), pltpu.roll(x, HEAD // 2, 1))
        sl = slice(p * LANES, (p + 1) * LANES)
        return x * cos_ref[rows, sl] + swapped * sin_ref[rows, sl]

    def split_heads(x):
        zero = jnp.zeros_like(x)
        return jnp.concatenate([jnp.where(low, x, zero), jnp.where(low, zero, x)], axis=0)

    q = [rotary(0, ci, p) for ci, p in tiles]
    k = [rotary(WIDTH, ci, p) * HEAD ** -0.5 for ci, p in tiles]
    v = [x_ref[0, ci * c:(ci + 1) * c, 2 * WIDTH + p * LANES:2 * WIDTH + (p + 1) * LANES] for ci, p in tiles]
    s = [_dot(split_heads(q_.astype(BF16)), k_.astype(BF16), NT) * dmat_ref[p]
         for q_, k_, (ci, p) in zip(q, k, tiles)]
    inner = [_dot(jnp.concatenate([s_[:c], s_[c:]], axis=1).astype(BF16), split_heads(v_)) for s_, v_ in zip(s, v)]
    kv = [jnp.where(same_head, _dot((k_ * zeta_ref[:, p * LANES:(p + 1) * LANES]).astype(BF16), v_, TN), 0.0)
          for k_, v_, (ci, p) in zip(k, v, tiles)]
    entering = []
    state = [state_ref[p] for p in range(N_PAIRS)]
    for ci in range(nc):
        for p in range(N_PAIRS):
            entering.append(state[p])
            state[p] = state[p] * cd_ref[p] + kv[ci * N_PAIRS + p]
    for p in range(N_PAIRS):
        state_ref[p] = state[p]
    for n, (ci, p) in enumerate(tiles):
        cross = _dot((q[n] * xi_ref[:, p * LANES:(p + 1) * LANES]).astype(BF16), entering[n].astype(BF16))
        acc_ref[ci * c:(ci + 1) * c, p * LANES:(p + 1) * LANES] = inner[n] + cross
    o = acc_ref[...]
    o = o * lax.rsqrt(_seg_sum(o * o, bd_ref[...], split=False) * (1.0 / HEAD) + NORM_EPS)
    g = x_ref[0, :, 3 * WIDTH:4 * WIDTH].astype(F32)
    o_ref[0] = (o * (g * jax.nn.sigmoid(g))).astype(o_ref.dtype)


def _retention(proj, tables, bd):
    b, t, _ = proj.shape
    c = RET_CHUNK
    tb = RET_BLOCK
    cos, sin, dmat, xi, zeta, cd = tables
    return pl.pallas_call(
        _ret_kernel,
        grid=(b, t // tb),
        in_specs=[
            pl.BlockSpec((1, tb, 4 * WIDTH), lambda bi, n: (bi, n, COL_RET // (4 * WIDTH))),
            pl.BlockSpec((tb, WIDTH), lambda bi, n: (n, 0)),
            pl.BlockSpec((tb, WIDTH), lambda bi, n: (n, 0)),
            pl.BlockSpec((N_PAIRS, 2 * c, c), lambda bi, n: (0, 0, 0)),
            pl.BlockSpec((c, WIDTH), lambda bi, n: (0, 0)),
            pl.BlockSpec((c, WIDTH), lambda bi, n: (0, 0)),
            pl.BlockSpec((N_PAIRS, LANES, LANES), lambda bi, n: (0, 0, 0)),
            pl.BlockSpec((LANES, LANES), lambda bi, n: (0, 0)),
        ],
        out_specs=pl.BlockSpec((1, tb, WIDTH), lambda bi, n: (bi, n, 0)),
        out_shape=jax.ShapeDtypeStruct((b, t, WIDTH), BF16),
        scratch_shapes=[pltpu.VMEM((N_PAIRS, LANES, LANES), F32), pltpu.VMEM((tb, WIDTH), F32)],
        compiler_params=_params("arbitrary", "arbitrary"),
        name="retention",
    )(proj, cos, sin, dmat, xi, zeta, cd, bd)


def _retention_tables(t):
    c = RET_CHUNK
    half = HEAD // 2
    inv_freq = 1.0 / (ROPE_BASE ** (jnp.arange(half, dtype=F32) * 2.0 / HEAD))
    ang = jnp.arange(t, dtype=F32)[:, None] * inv_freq[None, :]
    cos = jnp.tile(jnp.concatenate([jnp.cos(ang), jnp.cos(ang)], axis=1), (1, N_HEADS))
    sin = jnp.tile(jnp.concatenate([-jnp.sin(ang), jnp.sin(ang)], axis=1), (1, N_HEADS))
    log_gamma = jnp.log1p(-jnp.power(2.0, -5.0 - jnp.arange(N_HEADS, dtype=F32)))
    idx = jnp.arange(c, dtype=F32)
    diff = idx[:, None] - idx[None, :]
    dmat = jnp.where(diff >= 0, jnp.exp(log_gamma[:, None, None] * jnp.maximum(diff, 0.0)), 0.0)
    xi = jnp.exp(log_gamma[:, None] * (idx[None, :] + 1.0))
    zeta = jnp.exp(log_gamma[:, None] * (c - 1.0 - idx[None, :]))
    chunk_decay = jnp.exp(log_gamma * c)
    xi = jnp.repeat(xi.T, HEAD, axis=1)
    zeta = jnp.repeat(zeta.T, HEAD, axis=1)
    cd = jnp.broadcast_to(jnp.repeat(chunk_decay, HEAD).reshape(N_PAIRS, LANES, 1), (N_PAIRS, LANES, LANES))
    dmat = dmat.reshape(N_PAIRS, 2 * c, c)
    return cos, sin, dmat, xi, zeta, cd


def _bdot(a, b, dims=NN):
    return _dot(a.astype(BF16), b.astype(BF16), dims)


def _rwkv_kernel(ops_ref, eend_ref, lng_ref, lnb_ref, bd_ref, o_ref, state_ref, y_ref):
    L = RWKV_CHUNK
    tb = ops_ref.shape[1]
    nc = tb // L

    @pl.when(pl.program_id(1) == 0)
    def _():
        state_ref[...] = jnp.zeros_like(state_ref)

    bd = bd_ref[...]
    lane = lax.broadcasted_iota(jnp.int32, (L, LANES), 1)
    trow = lax.broadcasted_iota(jnp.int32, (L, LANES), 0)
    low = lane < HEAD
    tcol = lane & (HEAD - 1)
    strict = tcol < trow
    incl = tcol <= trow
    eye = (tcol == trow).astype(F32)
    rr = lax.broadcasted_iota(jnp.int32, (LANES, LANES), 0)
    cc = lax.broadcasted_iota(jnp.int32, (LANES, LANES), 1)
    same_head = (rr < HEAD) == (cc < HEAD)
    diag = rr == cc

    def bdiag(z):
        z = z.astype(BF16)
        zero = jnp.zeros_like(z)
        return jnp.concatenate([jnp.where(low, z, zero), jnp.where(low, zero, z)], axis=0)

    def bmm(x, z):
        return _dot(x.astype(BF16), bdiag(z))

    def stack(top, bottom):
        return jnp.concatenate([top, bottom], axis=0).astype(BF16)

    solved = {}

    def solve(chunks):
        tiles = [(c, p) for c in chunks for p in range(N_PAIRS)]

        def operand(slot):
            return [ops_ref[0, c * L:(c + 1) * L, slot * WIDTH + p * LANES:slot * WIDTH + (p + 1) * LANES]
                    for c, p in tiles]

        at_, rt_, bt_, kt_, vp_ = operand(OPS_A), operand(OPS_R), operand(OPS_B), operand(OPS_K), operand(OPS_V)
        bh, kh = operand(OPS_BH), operand(OPS_KH)
        tt = [_dot(stack(a_, r_), jnp.concatenate([bdiag(b_), bdiag(k_)], axis=0), NT)
              for a_, r_, b_, k_ in zip(at_, rt_, bt_, kt_)]
        yield
        t_ab = [jnp.where(strict, m[:L, :LANES], 0.0) for m in tt]
        t_rb = [jnp.where(incl, m[L:, :LANES], 0.0) for m in tt]
        t_ak = [jnp.where(strict, m[:L, LANES:], 0.0) for m in tt]
        t_rk = [jnp.where(incl, m[L:, LANES:], 0.0) for m in tt]
        inv = [eye + t for t in t_ab]
        pw = [bmm(t, t) for t in t_ab]
        yield
        for _ in range(4):
            st = [_dot(stack(w, i), bdiag(w)) for w, i in zip(pw, inv)]
            inv = [i + s[L:] for i, s in zip(inv, st)]
            pw = [s[:L] for s in st]
            yield
        inv = [i + bmm(i, w) for i, w in zip(inv, pw)]
        tvk = [_dot(stack(ta, tk), bdiag(v_)) for ta, tk, v_ in zip(t_ak, t_rk, vp_)]
        yield
        lhs = [stack(i, bmm(t, i)) for i, t in zip(inv, t_rb)]
        yield
        ra = [_dot(x, bdiag(a_)) for x, a_ in zip(lhs, at_)]
        ru = [_dot(x, bdiag(s[:L])) for x, s in zip(lhs, tvk)]
        yield
        qe = [r_.astype(F32) + s[L:] for r_, s in zip(rt_, ra)]
        yl = [s[L:] + s2[L:] for s, s2 in zip(ru, tvk)]
        e_end = [eend_ref[0, c, :, p * LANES:(p + 1) * LANES] for c, p in tiles]
        au = [_bdot(jnp.concatenate([s[:L], s2[:L]], axis=1), b_, TN) for s, s2, b_ in zip(ra, ru, bh)]
        pm = [jnp.where(same_head, s[:LANES], 0.0) + jnp.where(diag, e, 0.0) for s, e in zip(au, e_end)]
        gm = [jnp.where(same_head, s[LANES:] + _bdot(v_, k_, TN), 0.0) for s, v_, k_ in zip(au, vp_, kh)]
        for n, (c, p) in enumerate(tiles):
            solved.setdefault(c, {})[p] = (qe[n], yl[n], pm[n], gm[n])
        yield

    state = [state_ref[p] for p in range(N_PAIRS)]

    def scan_step(c):
        for p in range(N_PAIRS):
            qe, yl, pm, gm = solved[c][p]
            y_ref[c * L:(c + 1) * L, p * LANES:(p + 1) * LANES] = _bdot(qe, state[p], NT) + yl
            state[p] = _bdot(state[p], pm) + gm

    def finish(r0, r1):
        y = y_ref[r0:r1]
        mean = _seg_sum(y, bd) * (1.0 / HEAD)
        d = y - mean
        var = _seg_sum(d * d, bd, split=False) * (1.0 / HEAD)
        yn = d * lax.rsqrt(var + RWKV_LN_EPS) * lng_ref[...] + lnb_ref[...]
        bonus = ops_ref[0, r0:r1, OPS_BONUS * WIDTH:(OPS_BONUS + 1) * WIDTH].astype(F32)
        g = ops_ref[0, r0:r1, OPS_G * WIDTH:(OPS_G + 1) * WIDTH].astype(F32)
        o_ref[0, r0:r1] = ((yn + bonus) * g).astype(o_ref.dtype)

    half = nc // 2
    for _ in solve(range(half)):
        pass
    pending = list(range(half))
    for stage, _ in enumerate(solve(range(half, nc))):
        if pending and stage % 2 == 1:
            scan_step(pending.pop(0))
    for c in pending:
        scan_step(c)
    for j, c in enumerate(range(half, nc)):
        scan_step(c)
        finish(j * L, (j + 1) * L)
    for p in range(N_PAIRS):
        state_ref[p] = state[p]
    finish(half * L, tb)


def _rwkv(ops, eend, lnx_g, lnx_b, bd):
    b, t, _ = ops.shape
    tb = RWKV_BLOCK
    nc = tb // RWKV_CHUNK
    row = pl.BlockSpec((1, WIDTH), lambda bi, n: (0, 0))
    return pl.pallas_call(
        _rwkv_kernel,
        grid=(b, t // tb),
        in_specs=[
            pl.BlockSpec((1, tb, OPS_WIDTH), lambda bi, n: (bi, n, 0)),
            pl.BlockSpec((1, nc, 1, WIDTH), lambda bi, n: (bi, n, 0, 0)),
            row, row,
            pl.BlockSpec((LANES, LANES), lambda bi, n: (0, 0)),
        ],
        out_specs=pl.BlockSpec((1, tb, WIDTH), lambda bi, n: (bi, n, 0)),
        out_shape=jax.ShapeDtypeStruct((b, t, WIDTH), BF16),
        scratch_shapes=[
            pltpu.VMEM((N_PAIRS, LANES, LANES), F32),
            pltpu.VMEM((tb, WIDTH), F32),
        ],
        compiler_params=_params("arbitrary", "arbitrary"),
        name="rwkv7",
    )(ops, eend, lnx_g, lnx_b, bd)


def _merge_kernel(oa_ref, ob_ref, oc_ref, g0_ref, g1_ref, g2_ref, x_ref, wa_ref, wb_ref, wc_ref, wo_ref, o_ref):
    def branch(o, w, g):
        return jax.nn.sigmoid(g[...].astype(F32)) * _dot(o[...], w[...])

    mixed = branch(oa_ref, wa_ref, g0_ref) + branch(ob_ref, wb_ref, g1_ref) + branch(oc_ref, wc_ref, g2_ref)
    o_ref[...] = x_ref[...] + _dot(mixed.astype(BF16), wo_ref[...])


def _merge(x, proj, oa, ob, oc, wa, wb, wc, wo, tm):
    n, d = x.shape
    branch = pl.BlockSpec((tm, WIDTH), lambda i: (i, 0))
    gate = lambda j: pl.BlockSpec((tm, d), lambda i: (i, COL_GATES // d + j))
    return pl.pallas_call(
        _merge_kernel,
        grid=(n // tm,),
        in_specs=[branch, branch, branch, gate(0), gate(1), gate(2),
                  pl.BlockSpec((tm, d), lambda i: (i, 0)),
                  _resident((WIDTH, d)), _resident((WIDTH, d)), _resident((WIDTH, d)), _resident((d, d))],
        out_specs=pl.BlockSpec((tm, d), lambda i: (i, 0)),
        out_shape=jax.ShapeDtypeStruct((n, d), F32),
        compiler_params=_params("arbitrary"),
        name="merge_out_proj",
    )(oa, ob, oc, proj, proj, proj, x, wa, wb, wc, wo)


def _ffn_kernel(x_ref, g_ref, wg_ref, wu_ref, wd_ref, o_ref):
    x = x_ref[...]
    ms = jnp.mean(x * x, axis=-1, keepdims=True)
    h = (x * lax.rsqrt(ms + NORM_EPS) * g_ref[...]).astype(BF16)
    gate = _dot(h, wg_ref[...])
    up = _dot(h, wu_ref[...])
    act = (gate * jax.nn.sigmoid(gate) * up).astype(BF16)
    o_ref[...] = x + _dot(act, wd_ref[...])


def _ffn(x, g, wg, wu, wd, tm):
    n, d = x.shape
    f = wg.shape[1]
    return pl.pallas_call(
        _ffn_kernel,
        grid=(n // tm,),
        in_specs=[pl.BlockSpec((tm, d), lambda i: (i, 0)), _resident((1, d)),
                  _resident((d, f)), _resident((d, f)), _resident((f, d))],
        out_specs=pl.BlockSpec((tm, d), lambda i: (i, 0)),
        out_shape=jax.ShapeDtypeStruct((n, d), F32),
        compiler_params=_params("arbitrary"),
        name="swiglu_ffn",
    )(x, g, wg, wu, wd)


def _lora_weight(w2, a2, g2):
    w = jnp.zeros((2 * LANES, 3 * WIDTH), F32)
    w = w.at[0:64, 0:WIDTH].set(w2)
    w = w.at[64:128, WIDTH:2 * WIDTH].set(a2)
    w = w.at[128:256, 2 * WIDTH:3 * WIDTH].set(g2)
    return w.astype(BF16)


def _layer(x, consts, norm1_g, w_in, attn_q_norm_g, attn_k_norm_g, attn_sinks, w_attn_o,
           rwkv_shift_mu, rwkv_w0, rwkv_w2, rwkv_a0, rwkv_a2, rwkv_g2, rwkv_k_k, rwkv_k_a,
           rwkv_r_k, rwkv_lnx_g, rwkv_lnx_b, w_rwkv_o, w_ret_o, w_out,
           norm2_g, w_ffn_gate, w_ffn_up, w_ffn_down):
    b, t, d = x.shape
    n = b * t
    bd, tri, ret_tables = consts
    x2 = x.reshape(n, d)
    row = lambda a: a.reshape(1, WIDTH)
    prm = (rwkv_shift_mu.reshape(1, -1), row(rwkv_w0), row(rwkv_a0), row(rwkv_k_k), row(rwkv_k_a), row(rwkv_r_k),
           _lora_weight(rwkv_w2, rwkv_a2, rwkv_g2))
    tm = IN_PROJ_ROWS
    proj, ops, eend = _in_proj(x2, norm1_g.reshape(1, d), w_in.astype(BF16), prm, tri, bd, tm, t // tm)
    proj3 = proj.reshape(b, t, MAIN_WIDTH)

    o_a = _attention(proj3, attn_sinks, jnp.tile(attn_q_norm_g, N_HEADS).reshape(1, WIDTH),
                     jnp.tile(attn_k_norm_g, 2).reshape(1, LANES), bd, tq=512)

    o_b = _rwkv(ops.reshape(b, t, OPS_WIDTH), eend.reshape(b, t // RWKV_CHUNK, 1, WIDTH),
                row(rwkv_lnx_g), row(rwkv_lnx_b), bd)

    o_c = _retention(proj3, ret_tables, bd)

    x2 = _merge(x2, proj, o_a.reshape(n, WIDTH), o_b.reshape(n, WIDTH), o_c.reshape(n, WIDTH),
                w_attn_o.astype(BF16), w_rwkv_o.astype(BF16), w_ret_o.astype(BF16), w_out.astype(BF16), tm=512)
    x2 = _ffn(x2, norm2_g.reshape(1, d), w_ffn_gate.astype(BF16), w_ffn_up.astype(BF16),
              w_ffn_down.astype(BF16), tm=512)
    return x2.reshape(b, t, d)


def _constants(t):
    head_id = jnp.arange(LANES) // HEAD
    bd = (head_id[:, None] == head_id[None, :]).astype(BF16)
    idx = jnp.arange(RWKV_CHUNK)
    tri = (idx[None, :] <= idx[:, None]).astype(BF16)
    return bd, tri, _retention_tables(t)


def kernel(x, norm1_g, w_in, attn_q_norm_g, attn_k_norm_g, attn_sinks, w_attn_o, rwkv_shift_mu, rwkv_w0, rwkv_w2, rwkv_a0, rwkv_a2, rwkv_g2, rwkv_k_k, rwkv_k_a, rwkv_r_k, rwkv_lnx_g, rwkv_lnx_b, w_rwkv_o, w_ret_o, w_out, norm2_g, w_ffn_gate, w_ffn_up, w_ffn_down):
    consts = _constants(x.shape[1])
    weights = (norm1_g, w_in, attn_q_norm_g, attn_k_norm_g, attn_sinks, w_attn_o, rwkv_shift_mu, rwkv_w0,
               rwkv_w2, rwkv_a0, rwkv_a2, rwkv_g2, rwkv_k_k, rwkv_k_a, rwkv_r_k, rwkv_lnx_g, rwkv_lnx_b,
               w_rwkv_o, w_ret_o, w_out, norm2_g, w_ffn_gate, w_ffn_up, w_ffn_down)
    for layer in range(norm1_g.shape[0]):
        x = _layer(x, consts, *(w[layer] for w in weights))
    return x
```

```python
import functools

import jax
import jax.numpy as jnp
from jax import lax
from jax.experimental import pallas as pl
from jax.experimental.pallas import tpu as pltpu

F32 = jnp.float32
BF16 = jnp.bfloat16

HEAD = 64
LANES = 128
D_MODEL = 1024
N_HEADS = 8
WIDTH = N_HEADS * HEAD
N_PAIRS = WIDTH // LANES
ATTN_BLOCK = 128
RET_CHUNK = 128
RET_BLOCK = 512
RWKV_CHUNK = 64
RWKV_BLOCK = 512
NORM_EPS = 1e-6
RWKV_LN_EPS = 64e-5
ROPE_BASE = 10000.0
VMEM_LIMIT = 56 * 1024 * 1024

COL_RET = 0
COL_GATES = 2048
COL_ATTN_Q = 5120
COL_ATTN_K = 5632
COL_ATTN_V = 5760
MAIN_WIDTH = 5888
SRC_ATTN = 0
SRC_RWKV = 768
SRC_RET = 2560
RWKV_COLS = 1792
OPS_A, OPS_R, OPS_B, OPS_K, OPS_BH, OPS_KH, OPS_V, OPS_G, OPS_BONUS = range(9)
OPS_WIDTH = 9 * WIDTH

NN = (((1,), (0,)), ((), ()))
NT = (((1,), (1,)), ((), ()))
TN = (((0,), (0,)), ((), ()))


def _dot(a, b, dims=NN, precision=None):
    return lax.dot_general(a, b, dims, precision=precision, preferred_element_type=F32)


def _seg_sum(x, bd, split=True):
    cols = []
    for j in range(x.shape[1] // LANES):
        xc = x[:, j * LANES:(j + 1) * LANES]
        hi = xc.astype(BF16)
        s = _dot(hi, bd)
        if split:
            s = s + _dot((xc - hi.astype(F32)).astype(BF16), bd)
        cols.append(s)
    return cols[0] if len(cols) == 1 else jnp.concatenate(cols, axis=1)


def _params(*sem):
    return pltpu.CompilerParams(dimension_semantics=sem, vmem_limit_bytes=VMEM_LIMIT)


def _resident(shape):
    nd = len(shape)
    return pl.BlockSpec(shape, lambda *_: (0,) * nd, pipeline_mode=pl.Buffered(1))


IN_PROJ_ROWS = 512
PREP_ROWS = 256
PIECE = 256


def _main_pieces():
    pieces = [(SRC_RET + d, d) for d in range(0, COL_ATTN_Q, PIECE)]
    pieces += [(SRC_ATTN + d, COL_ATTN_Q + d) for d in range(0, MAIN_WIDTH - COL_ATTN_Q, PIECE)]
    return pieces


def _in_proj_kernel(x_ref, g_ref, w_ref, mu_ref, w0_ref, a0_ref, kk_ref, ka_ref, rk_ref, wl_ref, tri_ref, bd_ref,
                    main_ref, ops_ref, eend_ref, carry_ref, *, blocks_per_row):
    tm = x_ref.shape[0]
    L = RWKV_CHUNK

    @pl.when(pl.program_id(0) == 0)
    def _():
        carry_ref[...] = jnp.zeros_like(carry_ref)

    x = x_ref[...]
    ms = jnp.mean(x * x, axis=-1, keepdims=True)
    h = (x * lax.rsqrt(ms + NORM_EPS) * g_ref[...]).astype(BF16)
    hr = _dot(h, w_ref[:, SRC_RWKV:SRC_RWKV + RWKV_COLS])
    bd = bd_ref[...]
    tri = tri_ref[...]
    prev_row = jnp.where(pl.program_id(0) % blocks_per_row == 0, 0.0, carry_ref[0:1, :])
    carry_ref[0:1, :] = hr[tm - 1:tm]

    def put(rows, slot, val):
        ops_ref[rows, slot * WIDTH:(slot + 1) * WIDTH] = val.astype(BF16)

    pieces = iter(_main_pieces())

    def emit_pieces(count):
        for _ in range(count):
            piece = next(pieces, None)
            if piece is not None:
                src, dst = piece
                main_ref[:, dst:dst + PIECE] = _dot(h, w_ref[:, src:src + PIECE]).astype(main_ref.dtype)

    groups = tm // PREP_ROWS
    per_stage = -(-len(_main_pieces()) // (2 * groups))
    for gi in range(groups):
        r0 = gi * PREP_ROWS
        rows = slice(r0, r0 + PREP_ROWS)
        hb = hr[rows]
        row = lax.broadcasted_iota(jnp.int32, hb.shape, 0)
        prev = jnp.where(row == 0, prev_row, pltpu.roll(hb, 1, 0))
        prev_row = hb[PREP_ROWS - 1:PREP_ROWS]
        z = hb + mu_ref[...] * (prev - hb)
        r, k, v, lo = z[:, 0:WIDTH], z[:, WIDTH:2 * WIDTH], z[:, 2 * WIDTH:3 * WIDTH], z[:, 3 * WIDTH:]
        lane_lo = lax.broadcasted_iota(jnp.int32, lo.shape, 1)
        act = jnp.where(lane_lo < HEAD, jnp.tanh(lo), jnp.where(lane_lo < 2 * HEAD, lo, jax.nn.sigmoid(lo)))
        kkr = k * kk_ref[...]
        lora = _dot(act.astype(BF16), wl_ref[...])
        kk_norm = _seg_sum(kkr * kkr, bd, split=False)
        emit_pieces(per_stage)
        wx = -(w0_ref[...] + lora[:, 0:WIDTH])
        softplus = jnp.maximum(wx, 0.0) + jnp.log(1.0 + jnp.exp(-jnp.abs(wx)))
        lw = -jnp.exp(-softplus - 0.5)
        a = jax.nn.sigmoid(a0_ref[...] + lora[:, WIDTH:2 * WIDTH])
        kk = kkr / jnp.maximum(jnp.sqrt(kk_norm), 1e-12)
        k2 = k * (1.0 + (a - 1.0) * ka_ref[...])
        lw_hi = lw.astype(BF16)
        lw_lo = (lw - lw_hi.astype(F32)).astype(BF16)
        chunks = [slice(c0, c0 + L) for c0 in range(0, PREP_ROWS, L)]
        cums = [_dot(tri, lw_hi[cs]) + _dot(tri, lw_lo[cs]) for cs in chunks]
        bonus = _seg_sum(r * k2 * rk_ref[...], bd) * v
        emit_pieces(per_stage)
        put(rows, OPS_V, v)
        put(rows, OPS_G, lora[:, 2 * WIDTH:3 * WIDTH])
        put(rows, OPS_BONUS, bonus)
        for cs, cum in zip(chunks, cums):
            crow = slice(r0 + cs.start, r0 + cs.stop)
            e_pos = jnp.exp(cum)
            e_neg = jnp.exp(-cum)
            e_end = e_pos[L - 1:L]
            b_t = kk[cs] * a[cs] * e_neg
            k_t = k2[cs] * e_neg
            put(crow, OPS_A, -kk[cs] * jnp.exp(cum - lw[cs]))
            put(crow, OPS_R, r[cs] * e_pos)
            put(crow, OPS_B, b_t)
            put(crow, OPS_K, k_t)
            put(crow, OPS_BH, b_t * e_end)
            put(crow, OPS_KH, k_t * e_end)
            eend_ref[crow.start // L:crow.start // L + 1, :] = e_end
    emit_pieces(len(_main_pieces()))


def _in_proj(x, g, w, prm, tri, bd, tm, blocks_per_row):
    n, d = x.shape
    row = lambda width: _resident((1, width))
    return pl.pallas_call(
        functools.partial(_in_proj_kernel, blocks_per_row=blocks_per_row),
        grid=(n // tm,),
        in_specs=[pl.BlockSpec((tm, d), lambda i: (i, 0)), row(d), _resident(w.shape),
                  row(RWKV_COLS), row(WIDTH), row(WIDTH), row(WIDTH), row(WIDTH), row(WIDTH),
                  _resident((2 * LANES, 3 * WIDTH)), _resident((RWKV_CHUNK, RWKV_CHUNK)), _resident((LANES, LANES))],
        out_specs=[pl.BlockSpec((tm, MAIN_WIDTH), lambda i: (i, 0)),
                   pl.BlockSpec((tm, OPS_WIDTH), lambda i: (i, 0)),
                   pl.BlockSpec((tm // RWKV_CHUNK, WIDTH), lambda i: (i, 0))],
        out_shape=[jax.ShapeDtypeStruct((n, MAIN_WIDTH), BF16),
                   jax.ShapeDtypeStruct((n, OPS_WIDTH), BF16),
                   jax.ShapeDtypeStruct((n // RWKV_CHUNK, WIDTH), F32)],
        scratch_shapes=[pltpu.VMEM((8, RWKV_COLS), F32)],
        compiler_params=_params("arbitrary"),
        name="in_proj",
    )(x, g, w, *prm, tri, bd)


def _attn_kernel(sink_ref, q_ref, kc_ref, vc_ref, kp_ref, vp_ref, gq_ref, gk_ref, bd_ref, o_ref, *, tq):
    i = pl.program_id(1)
    bd = bd_ref[...]
    q = q_ref[0].astype(F32)
    qn = q * lax.rsqrt(_seg_sum(q * q, bd, split=False) * (1.0 / HEAD) + NORM_EPS) * (gq_ref[...] * HEAD ** -0.5)
    k = jnp.concatenate([kp_ref[0], kc_ref[0]], axis=0).astype(F32)
    kn = k * lax.rsqrt(_seg_sum(k * k, bd, split=False) * (1.0 / HEAD) + NORM_EPS) * gk_ref[...]
    v = jnp.concatenate([vp_ref[0], vc_ref[0]], axis=0).astype(F32)

    lane_k = lax.broadcasted_iota(jnp.int32, k.shape, 1)
    lane_q = lax.broadcasted_iota(jnp.int32, (ATTN_BLOCK, LANES), 1)
    low_q = lane_q < HEAD
    key_row = lax.broadcasted_iota(jnp.int32, (ATTN_BLOCK, 2 * ATTN_BLOCK), 0)
    query_col = lax.broadcasted_iota(jnp.int32, (ATTN_BLOCK, 2 * ATTN_BLOCK), 1)
    from_prev = key_row > (query_col & (ATTN_BLOCK - 1))
    first_head = lax.broadcasted_iota(jnp.int32, (1, 2 * ATTN_BLOCK), 1) < ATTN_BLOCK

    k_dup, v_lo, v_hi = [], [], []
    for e in range(2):
        in_e = (lane_k >= e * HEAD) & (lane_k < (e + 1) * HEAD)
        k_e = jnp.where(in_e, kn, 0.0)
        k_dup.append((k_e + pltpu.roll(k_e, HEAD, 1)).astype(BF16))
        v_e = jnp.where(in_e, v, 0.0)
        lo = v_e if e == 0 else pltpu.roll(v_e, HEAD, 1)
        v_lo.append(lo.astype(BF16))
        v_hi.append(pltpu.roll(lo, HEAD, 1).astype(BF16))

    nb = tq // ATTN_BLOCK
    tiles = [(n, c) for n in range(nb) for c in range(N_PAIRS)]

    def keys(arr, n, c):
        return arr[c // 2][n * ATTN_BLOCK:(n + 2) * ATTN_BLOCK]

    qs = []
    for n, c in tiles:
        qc = qn[n * ATTN_BLOCK:(n + 1) * ATTN_BLOCK, c * LANES:(c + 1) * LANES]
        qs.append(jnp.concatenate([jnp.where(low_q, qc, 0.0), jnp.where(low_q, 0.0, qc)], axis=0).astype(BF16))
    s = [_dot(keys(k_dup, n, c), q_, NT) for q_, (n, c) in zip(qs, tiles)]
    no_prev = jnp.where(i == 0, -jnp.inf, 0.0)
    s = [jnp.where(from_prev, s_[:ATTN_BLOCK] + no_prev if n == 0 else s_[:ATTN_BLOCK], s_[ATTN_BLOCK:])
         for s_, (n, c) in zip(s, tiles)]
    sink = [jnp.where(first_head, sink_ref[2 * c], sink_ref[2 * c + 1]) for n, c in tiles]
    m = [jnp.maximum(jnp.max(s_, axis=0, keepdims=True), k_) for s_, k_ in zip(s, sink)]
    p = [jnp.exp(s_ - m_) for s_, m_ in zip(s, m)]
    den = [jnp.sum(p_, axis=0, keepdims=True) + jnp.exp(k_ - m_) for p_, k_, m_ in zip(p, sink, m)]
    p = [(p_ * (1.0 / d_)).astype(BF16) for p_, d_ in zip(p, den)]
    zero = jnp.zeros_like(p[0])
    pb = [jnp.concatenate([jnp.where(from_prev, p_, zero), jnp.where(from_prev, zero, p_)], axis=0) for p_ in p]
    for p_, (n, c) in zip(pb, tiles):
        o = _dot(p_[:, :ATTN_BLOCK], keys(v_lo, n, c), TN) + _dot(p_[:, ATTN_BLOCK:], keys(v_hi, n, c), TN)
        o_ref[0, n * ATTN_BLOCK:(n + 1) * ATTN_BLOCK, c * LANES:(c + 1) * LANES] = o.astype(o_ref.dtype)


def _attention(proj, sinks, gq, gk, bd, tq):
    b, t, _ = proj.shape
    nb = tq // ATTN_BLOCK
    kernel = functools.partial(_attn_kernel, tq=tq)
    return pl.pallas_call(
        kernel,
        grid=(b, t // tq),
        in_specs=[
            pl.BlockSpec(memory_space=pltpu.SMEM),
            pl.BlockSpec((1, tq, WIDTH), lambda bi, i: (bi, i, COL_ATTN_Q // WIDTH)),
            pl.BlockSpec((1, tq, LANES), lambda bi, i: (bi, i, COL_ATTN_K // LANES)),
            pl.BlockSpec((1, tq, LANES), lambda bi, i: (bi, i, COL_ATTN_V // LANES)),
            pl.BlockSpec((1, ATTN_BLOCK, LANES),
                         lambda bi, i: (bi, jnp.maximum(i * nb - 1, 0), COL_ATTN_K // LANES)),
            pl.BlockSpec((1, ATTN_BLOCK, LANES),
                         lambda bi, i: (bi, jnp.maximum(i * nb - 1, 0), COL_ATTN_V // LANES)),
            pl.BlockSpec((1, WIDTH), lambda bi, i: (0, 0)),
            pl.BlockSpec((1, LANES), lambda bi, i: (0, 0)),
            pl.BlockSpec((LANES, LANES), lambda bi, i: (0, 0)),
        ],
        out_specs=pl.BlockSpec((1, tq, WIDTH), lambda bi, i: (bi, i, 0)),
        out_shape=jax.ShapeDtypeStruct((b, t, WIDTH), BF16),
        compiler_params=_params("arbitrary", "arbitrary"),
        name="swa_attention",
    )(sinks, proj, proj, proj, proj, proj, gq, gk, bd)


def _ret_kernel(x_ref, cos_ref, sin_ref, dmat_ref, xi_ref, zeta_ref, cd_ref, bd_ref, o_ref, state_ref, acc_ref):
    @pl.when(pl.program_id(1) == 0)
    def _():
        state_ref[...] = jnp.zeros_like(state_ref)

    c = RET_CHUNK
    nc = x_ref.shape[1] // c
    lane = lax.broadcasted_iota(jnp.int32, (c, LANES), 1)
    first_half = (lane & (HEAD - 1)) < HEAD // 2
    low = lane < HEAD
    rr = lax.broadcasted_iota(jnp.int32, (LANES, LANES), 0)
    cc = lax.broadcasted_iota(jnp.int32, (LANES, LANES), 1)
    same_head = (rr < HEAD) == (cc < HEAD)
    tiles = [(ci, p) for ci in range(nc) for p in range(N_PAIRS)]

    def rotary(col, ci, p):
        rows = slice(ci * c, (ci + 1) * c)
        x = x_ref[0, rows, col + p * LANES:col + (p + 1) * LANES].astype(F32)
        swapped = jnp.where(first_half, pltpu.roll(x, LANES - HEAD // 2, 1), pltpu.roll(x, HEAD // 2, 1))
        sl = slice(p * LANES, (p + 1) * LANES)
        return x * cos_ref[rows, sl] + swapped * sin_ref[rows, sl]

    def split_heads(x):
        zero = jnp.zeros_like(x)
        return jnp.concatenate([jnp.where(low, x, zero), jnp.where(low, zero, x)], axis=0)

    q = [rotary(0, ci, p) for ci, p in tiles]
    k = [rotary(WIDTH, ci, p) * HEAD ** -0.5 for ci, p in tiles]
    v = [x_ref[0, ci * c:(ci + 1) * c, 2 * WIDTH + p * LANES:2 * WIDTH + (p + 1) * LANES] for ci, p in tiles]
    s = [_dot(split_heads(q_.astype(BF16)), k_.astype(BF16), NT) * dmat_ref[p]
         for q_, k_, (ci, p) in zip(q, k, tiles)]
    inner = [_dot(jnp.concatenate([s_[:c], s_[c:]], axis=1).astype(BF16), split_heads(v_)) for s_, v_ in zip(s, v)]
    kv = [jnp.where(same_head, _dot((k_ * zeta_ref[:, p * LANES:(p + 1) * LANES]).astype(BF16), v_, TN), 0.0)
          for k_, v_, (ci, p) in zip(k, v, tiles)]
    entering = []
    state = [state_ref[p] for p in range(N_PAIRS)]
    for ci in range(nc):
        for p in range(N_PAIRS):
            entering.append(state[p])
            state[p] = state[p] * cd_ref[p] + kv[ci * N_PAIRS + p]
    for p in range(N_PAIRS):
        state_ref[p] = state[p]
    for n, (ci, p) in enumerate(tiles):
        cross = _dot((q[n] * xi_ref[:, p * LANES:(p + 1) * LANES]).astype(BF16), entering[n].astype(BF16))
        acc_ref[ci * c:(ci + 1) * c, p * LANES:(p + 1) * LANES] = inner[n] + cross
    o = acc_ref[...]
    o = o * lax.rsqrt(_seg_sum(o * o, bd_ref[...], split=False) * (1.0 / HEAD) + NORM_EPS)
    g = x_ref[0, :, 3 * WIDTH:4 * WIDTH].astype(F32)
    o_ref[0] = (o * (g * jax.nn.sigmoid(g))).astype(o_ref.dtype)


def _retention(proj, tables, bd):
    b, t, _ = proj.shape
    c = RET_CHUNK
    tb = RET_BLOCK
    cos, sin, dmat, xi, zeta, cd = tables
    return pl.pallas_call(
        _ret_kernel,
        grid=(b, t // tb),
        in_specs=[
            pl.BlockSpec((1, tb, 4 * WIDTH), lambda bi, n: (bi, n, COL_RET // (4 * WIDTH))),
            pl.BlockSpec((tb, WIDTH), lambda bi, n: (n, 0)),
            pl.BlockSpec((tb, WIDTH), lambda bi, n: (n, 0)),
            pl.BlockSpec((N_PAIRS, 2 * c, c), lambda bi, n: (0, 0, 0)),
            pl.BlockSpec((c, WIDTH), lambda bi, n: (0, 0)),
            pl.BlockSpec((c, WIDTH), lambda bi, n: (0, 0)),
            pl.BlockSpec((N_PAIRS, LANES, LANES), lambda bi, n: (0, 0, 0)),
            pl.BlockSpec((LANES, LANES), lambda bi, n: (0, 0)),
        ],
        out_specs=pl.BlockSpec((1, tb, WIDTH), lambda bi, n: (bi, n, 0)),
        out_shape=jax.ShapeDtypeStruct((b, t, WIDTH), BF16),
        scratch_shapes=[pltpu.VMEM((N_PAIRS, LANES, LANES), F32), pltpu.VMEM((tb, WIDTH), F32)],
        compiler_params=_params("arbitrary", "arbitrary"),
        name="retention",
    )(proj, cos, sin, dmat, xi, zeta, cd, bd)


def _retention_tables(t):
    c = RET_CHUNK
    half = HEAD // 2
    inv_freq = 1.0 / (ROPE_BASE ** (jnp.arange(half, dtype=F32) * 2.0 / HEAD))
    ang = jnp.arange(t, dtype=F32)[:, None] * inv_freq[None, :]
    cos = jnp.tile(jnp.concatenate([jnp.cos(ang), jnp.cos(ang)], axis=1), (1, N_HEADS))
    sin = jnp.tile(jnp.concatenate([-jnp.sin(ang), jnp.sin(ang)], axis=1), (1, N_HEADS))
    log_gamma = jnp.log1p(-jnp.power(2.0, -5.0 - jnp.arange(N_HEADS, dtype=F32)))
    idx = jnp.arange(c, dtype=F32)
    diff = idx[:, None] - idx[None, :]
    dmat = jnp.where(diff >= 0, jnp.exp(log_gamma[:, None, None] * jnp.maximum(diff, 0.0)), 0.0)
    xi = jnp.exp(log_gamma[:, None] * (idx[None, :] + 1.0))
    zeta = jnp.exp(log_gamma[:, None] * (c - 1.0 - idx[None, :]))
    chunk_decay = jnp.exp(log_gamma * c)
    xi = jnp.repeat(xi.T, HEAD, axis=1)
    zeta = jnp.repeat(zeta.T, HEAD, axis=1)
    cd = jnp.broadcast_to(jnp.repeat(chunk_decay, HEAD).reshape(N_PAIRS, LANES, 1), (N_PAIRS, LANES, LANES))
    dmat = dmat.reshape(N_PAIRS, 2 * c, c)
    return cos, sin, dmat, xi, zeta, cd


def _bdot(a, b, dims=NN):
    return _dot(a.astype(BF16), b.astype(BF16), dims)


def _rwkv_kernel(ops_ref, eend_ref, lng_ref, lnb_ref, bd_ref, o_ref, state_ref, y_ref):
    L = RWKV_CHUNK
    tb = ops_ref.shape[1]
    nc = tb // L

    @pl.when(pl.program_id(1) == 0)
    def _():
        state_ref[...] = jnp.zeros_like(state_ref)

    bd = bd_ref[...]
    lane = lax.broadcasted_iota(jnp.int32, (L, LANES), 1)
    trow = lax.broadcasted_iota(jnp.int32, (L, LANES), 0)
    low = lane < HEAD
    tcol = lane & (HEAD - 1)
    strict = tcol < trow
    incl = tcol <= trow
    eye = (tcol == trow).astype(F32)
    rr = lax.broadcasted_iota(jnp.int32, (LANES, LANES), 0)
    cc = lax.broadcasted_iota(jnp.int32, (LANES, LANES), 1)
    same_head = (rr < HEAD) == (cc < HEAD)
    diag = rr == cc

    def bdiag(z):
        z = z.astype(BF16)
        zero = jnp.zeros_like(z)
        return jnp.concatenate([jnp.where(low, z, zero), jnp.where(low, zero, z)], axis=0)

    def bmm(x, z):
        return _dot(x.astype(BF16), bdiag(z))

    def stack(top, bottom):
        return jnp.concatenate([top, bottom], axis=0).astype(BF16)

    solved = {}

    def solve(chunks):
        tiles = [(c, p) for c in chunks for p in range(N_PAIRS)]

        def operand(slot):
            return [ops_ref[0, c * L:(c + 1) * L, slot * WIDTH + p * LANES:slot * WIDTH + (p + 1) * LANES]
                    for c, p in tiles]

        at_, rt_, bt_, kt_, vp_ = operand(OPS_A), operand(OPS_R), operand(OPS_B), operand(OPS_K), operand(OPS_V)
        bh, kh = operand(OPS_BH), operand(OPS_KH)
        tt = [_dot(stack(a_, r_), jnp.concatenate([bdiag(b_), bdiag(k_)], axis=0), NT)
              for a_, r_, b_, k_ in zip(at_, rt_, bt_, kt_)]
        yield
        t_ab = [jnp.where(strict, m[:L, :LANES], 0.0) for m in tt]
        t_rb = [jnp.where(incl, m[L:, :LANES], 0.0) for m in tt]
        t_ak = [jnp.where(strict, m[:L, LANES:], 0.0) for m in tt]
        t_rk = [jnp.where(incl, m[L:, LANES:], 0.0) for m in tt]
        inv = [eye + t for t in t_ab]
        pw = [bmm(t, t) for t in t_ab]
        yield
        for _ in range(4):
            st = [_dot(stack(w, i), bdiag(w)) for w, i in zip(pw, inv)]
            inv = [i + s[L:] for i, s in zip(inv, st)]
            pw = [s[:L] for s in st]
            yield
        inv = [i + bmm(i, w) for i, w in zip(inv, pw)]
        tvk = [_dot(stack(ta, tk), bdiag(v_)) for ta, tk, v_ in zip(t_ak, t_rk, vp_)]
        yield
        lhs = [stack(i, bmm(t, i)) for i, t in zip(inv, t_rb)]
        yield
        ra = [_dot(x, bdiag(a_)) for x, a_ in zip(lhs, at_)]
        ru = [_dot(x, bdiag(s[:L])) for x, s in zip(lhs, tvk)]
        yield
        qe = [r_.astype(F32) + s[L:] for r_, s in zip(rt_, ra)]
        yl = [s[L:] + s2[L:] for s, s2 in zip(ru, tvk)]
        e_end = [eend_ref[0, c, :, p * LANES:(p + 1) * LANES] for c, p in tiles]
        au = [_bdot(jnp.concatenate([s[:L], s2[:L]], axis=1), b_, TN) for s, s2, b_ in zip(ra, ru, bh)]
        pm = [jnp.where(same_head, s[:LANES], 0.0) + jnp.where(diag, e, 0.0) for s, e in zip(au, e_end)]
        gm = [jnp.where(same_head, s[LANES:] + _bdot(v_, k_, TN), 0.0) for s, v_, k_ in zip(au, vp_, kh)]
        for n, (c, p) in enumerate(tiles):
            solved.setdefault(c, {})[p] = (qe[n], yl[n], pm[n], gm[n])
        yield

    state = [state_ref[p] for p in range(N_PAIRS)]

    def scan_step(c):
        for p in range(N_PAIRS):
            qe, yl, pm, gm = solved[c][p]
            y_ref[c * L:(c + 1) * L, p * LANES:(p + 1) * LANES] = _bdot(qe, state[p], NT) + yl
            state[p] = _bdot(state[p], pm) + gm

    def finish(c):
        rows = slice(c * L, (c + 1) * L)
        y = y_ref[rows]
        mean = _seg_sum(y, bd) * (1.0 / HEAD)
        yield
        d = y - mean
        var = _seg_sum(d * d, bd, split=False) * (1.0 / HEAD)
        yield
        yn = d * lax.rsqrt(var + RWKV_LN_EPS) * lng_ref[...] + lnb_ref[...]
        bonus = ops_ref[0, rows, OPS_BONUS * WIDTH:(OPS_BONUS + 1) * WIDTH].astype(F32)
        g = ops_ref[0, rows, OPS_G * WIDTH:(OPS_G + 1) * WIDTH].astype(F32)
        o_ref[0, rows] = ((yn + bonus) * g).astype(o_ref.dtype)

    active = []

    def tick(new=None):
        if new is not None:
            active.append(new)
        for work in list(active):
            if next(work, "done") == "done":
                active.remove(work)

    half = nc // 2
    for _ in solve(range(half)):
        pass
    pending = list(range(half))
    for stage, _ in enumerate(solve(range(half, nc))):
        if pending and stage % 2 == 1:
            scan_step(pending.pop(0))
    for c in pending:
        scan_step(c)
    for j, c in enumerate(range(half, nc)):
        scan_step(c)
        tick(finish(j))
    for p in range(N_PAIRS):
        state_ref[p] = state[p]
    while active:
        tick()
    active.extend(finish(c) for c in range(half, nc))
    while active:
        tick()


def _rwkv(ops, eend, lnx_g, lnx_b, bd):
    b, t, _ = ops.shape
    tb = RWKV_BLOCK
    nc = tb // RWKV_CHUNK
    row = pl.BlockSpec((1, WIDTH), lambda bi, n: (0, 0))
    return pl.pallas_call(
        _rwkv_kernel,
        grid=(b, t // tb),
        in_specs=[
            pl.BlockSpec((1, tb, OPS_WIDTH), lambda bi, n: (bi, n, 0)),
            pl.BlockSpec((1, nc, 1, WIDTH), lambda bi, n: (bi, n, 0, 0)),
            row, row,
            pl.BlockSpec((LANES, LANES), lambda bi, n: (0, 0)),
        ],
        out_specs=pl.BlockSpec((1, tb, WIDTH), lambda bi, n: (bi, n, 0)),
        out_shape=jax.ShapeDtypeStruct((b, t, WIDTH), BF16),
        scratch_shapes=[
            pltpu.VMEM((N_PAIRS, LANES, LANES), F32),
            pltpu.VMEM((tb, WIDTH), F32),
        ],
        compiler_params=_params("arbitrary", "arbitrary"),
        name="rwkv7",
    )(ops, eend, lnx_g, lnx_b, bd)


def _merge_kernel(oa_ref, ob_ref, oc_ref, g0_ref, g1_ref, g2_ref, x_ref, wa_ref, wb_ref, wc_ref, wo_ref, o_ref):
    def branch(o, w, g):
        return jax.nn.sigmoid(g[...].astype(F32)) * _dot(o[...], w[...])

    mixed = branch(oa_ref, wa_ref, g0_ref) + branch(ob_ref, wb_ref, g1_ref) + branch(oc_ref, wc_ref, g2_ref)
    o_ref[...] = x_ref[...] + _dot(mixed.astype(BF16), wo_ref[...])


def _merge(x, proj, oa, ob, oc, wa, wb, wc, wo, tm):
    n, d = x.shape
    branch = pl.BlockSpec((tm, WIDTH), lambda i: (i, 0))
    gate = lambda j: pl.BlockSpec((tm, d), lambda i: (i, COL_GATES // d + j))
    return pl.pallas_call(
        _merge_kernel,
        grid=(n // tm,),
        in_specs=[branch, branch, branch, gate(0), gate(1), gate(2),
                  pl.BlockSpec((tm, d), lambda i: (i, 0)),
                  _resident((WIDTH, d)), _resident((WIDTH, d)), _resident((WIDTH, d)), _resident((d, d))],
        out_specs=pl.BlockSpec((tm, d), lambda i: (i, 0)),
        out_shape=jax.ShapeDtypeStruct((n, d), F32),
        compiler_params=_params("arbitrary"),
        name="merge_out_proj",
    )(oa, ob, oc, proj, proj, proj, x, wa, wb, wc, wo)


def _ffn_kernel(x_ref, g_ref, wg_ref, wu_ref, wd_ref, o_ref):
    x = x_ref[...]
    ms = jnp.mean(x * x, axis=-1, keepdims=True)
    h = (x * lax.rsqrt(ms + NORM_EPS) * g_ref[...]).astype(BF16)
    gate = _dot(h, wg_ref[...])
    up = _dot(h, wu_ref[...])
    act = (gate * jax.nn.sigmoid(gate) * up).astype(BF16)
    o_ref[...] = x + _dot(act, wd_ref[...])


def _ffn(x, g, wg, wu, wd, tm):
    n, d = x.shape
    f = wg.shape[1]
    return pl.pallas_call(
        _ffn_kernel,
        grid=(n // tm,),
        in_specs=[pl.BlockSpec((tm, d), lambda i: (i, 0)), _resident((1, d)),
                  _resident((d, f)), _resident((d, f)), _resident((f, d))],
        out_specs=pl.BlockSpec((tm, d), lambda i: (i, 0)),
        out_shape=jax.ShapeDtypeStruct((n, d), F32),
        compiler_params=_params("arbitrary"),
        name="swiglu_ffn",
    )(x, g, wg, wu, wd)


def _lora_weight(w2, a2, g2):
    w = jnp.zeros((2 * LANES, 3 * WIDTH), F32)
    w = w.at[0:64, 0:WIDTH].set(w2)
    w = w.at[64:128, WIDTH:2 * WIDTH].set(a2)
    w = w.at[128:256, 2 * WIDTH:3 * WIDTH].set(g2)
    return w.astype(BF16)


def _layer(x, consts, norm1_g, w_in, attn_q_norm_g, attn_k_norm_g, attn_sinks, w_attn_o,
           rwkv_shift_mu, rwkv_w0, rwkv_w2, rwkv_a0, rwkv_a2, rwkv_g2, rwkv_k_k, rwkv_k_a,
           rwkv_r_k, rwkv_lnx_g, rwkv_lnx_b, w_rwkv_o, w_ret_o, w_out,
           norm2_g, w_ffn_gate, w_ffn_up, w_ffn_down):
    b, t, d = x.shape
    n = b * t
    bd, tri, ret_tables = consts
    x2 = x.reshape(n, d)
    row = lambda a: a.reshape(1, WIDTH)
    prm = (rwkv_shift_mu.reshape(1, -1), row(rwkv_w0), row(rwkv_a0), row(rwkv_k_k), row(rwkv_k_a), row(rwkv_r_k),
           _lora_weight(rwkv_w2, rwkv_a2, rwkv_g2))
    tm = IN_PROJ_ROWS
    proj, ops, eend = _in_proj(x2, norm1_g.reshape(1, d), w_in.astype(BF16), prm, tri, bd, tm, t // tm)
    proj3 = proj.reshape(b, t, MAIN_WIDTH)

    o_a = _attention(proj3, attn_sinks, jnp.tile(attn_q_norm_g, N_HEADS).reshape(1, WIDTH),
                     jnp.tile(attn_k_norm_g, 2).reshape(1, LANES), bd, tq=512)

    o_b = _rwkv(ops.reshape(b, t, OPS_WIDTH), eend.reshape(b, t // RWKV_CHUNK, 1, WIDTH),
                row(rwkv_lnx_g), row(rwkv_lnx_b), bd)

    o_c = _retention(proj3, ret_tables, bd)

    x2 = _merge(x2, proj, o_a.reshape(n, WIDTH), o_b.reshape(n, WIDTH), o_c.reshape(n, WIDTH),
                w_attn_o.astype(BF16), w_rwkv_o.astype(BF16), w_ret_o.astype(BF16), w_out.astype(BF16), tm=512)
    x2 = _ffn(x2, norm2_g.reshape(1, d), w_ffn_gate.astype(BF16), w_ffn_up.astype(BF16),
              w_ffn_down.astype(BF16), tm=512)
    return x2.reshape(b, t, d)


def _constants(t):
    head_id = jnp.arange(LANES) // HEAD
    bd = (head_id[:, None] == head_id[None, :]).astype(BF16)
    idx = jnp.arange(RWKV_CHUNK)
    tri = (idx[None, :] <= idx[:, None]).astype(BF16)
    return bd, tri, _retention_tables(t)


def kernel(x, norm1_g, w_in, attn_q_norm_g, attn_k_norm_g, attn_sinks, w_attn_o, rwkv_shift_mu, rwkv_w0, rwkv_w2, rwkv_a0, rwkv_a2, rwkv_g2, rwkv_k_k, rwkv_k_a, rwkv_r_k, rwkv_lnx_g, rwkv_lnx_b, w_rwkv_o, w_ret_o, w_out, norm2_g, w_ffn_gate, w_ffn_up, w_ffn_down):
    consts = _constants(x.shape[1])
    weights = (norm1_g, w_in, attn_q_norm_g, attn_k_norm_g, attn_sinks, w_attn_o, rwkv_shift_mu, rwkv_w0,
               rwkv_w2, rwkv_a0, rwkv_a2, rwkv_g2, rwkv_k_k, rwkv_k_a, rwkv_r_k, rwkv_lnx_g, rwkv_lnx_b,
               w_rwkv_o, w_ret_o, w_out, norm2_g, w_ffn_gate, w_ffn_up, w_ffn_down)
    for layer in range(norm1_g.shape[0]):
        x = _layer(x, consts, *(w[layer] for w in weights))
    return x
```

```python
import functools

import jax
import jax.numpy as jnp
from jax import lax
from jax.experimental import pallas as pl
from jax.experimental.pallas import tpu as pltpu

F32 = jnp.float32
BF16 = jnp.bfloat16

HEAD = 64
LANES = 128
D_MODEL = 1024
N_HEADS = 8
WIDTH = N_HEADS * HEAD
N_PAIRS = WIDTH // LANES
ATTN_BLOCK = 128
RET_CHUNK = 128
RET_BLOCK = 512
RWKV_CHUNK = 64
RWKV_BLOCK = 512
NORM_EPS = 1e-6
RWKV_LN_EPS = 64e-5
ROPE_BASE = 10000.0
VMEM_LIMIT = 56 * 1024 * 1024

COL_RET = 0
COL_GATES = 2048
COL_ATTN_Q = 5120
COL_ATTN_K = 5632
COL_ATTN_V = 5760
MAIN_WIDTH = 5888
SRC_ATTN = 0
SRC_RWKV = 768
SRC_RET = 2560
RWKV_COLS = 1792
OPS_A, OPS_R, OPS_B, OPS_K, OPS_BH, OPS_KH, OPS_V, OPS_G, OPS_BONUS = range(9)
OPS_WIDTH = 9 * WIDTH

NN = (((1,), (0,)), ((), ()))
NT = (((1,), (1,)), ((), ()))
TN = (((0,), (0,)), ((), ()))


def _dot(a, b, dims=NN, precision=None):
    return lax.dot_general(a, b, dims, precision=precision, preferred_element_type=F32)


def _seg_sum(x, bd, split=True):
    cols = []
    for j in range(x.shape[1] // LANES):
        xc = x[:, j * LANES:(j + 1) * LANES]
        hi = xc.astype(BF16)
        s = _dot(hi, bd)
        if split:
            s = s + _dot((xc - hi.astype(F32)).astype(BF16), bd)
        cols.append(s)
    return cols[0] if len(cols) == 1 else jnp.concatenate(cols, axis=1)


def _params(*sem):
    return pltpu.CompilerParams(dimension_semantics=sem, vmem_limit_bytes=VMEM_LIMIT)


def _resident(shape, layer=None):
    nd = len(shape)
    if layer is None:
        return pl.BlockSpec(shape, lambda *_: (0,) * nd, pipeline_mode=pl.Buffered(1))
    return pl.BlockSpec((None,) + tuple(shape), lambda *_: (layer,) + (0,) * nd, pipeline_mode=pl.Buffered(1))


IN_PROJ_ROWS = 512
PREP_ROWS = 256
PIECE = 256


def _main_pieces():
    pieces = [(SRC_RET + d, d) for d in range(0, COL_ATTN_Q, PIECE)]
    pieces += [(SRC_ATTN + d, COL_ATTN_Q + d) for d in range(0, MAIN_WIDTH - COL_ATTN_Q, PIECE)]
    return pieces


def _in_proj_kernel(x_ref, g_ref, w_ref, mu_ref, w0_ref, a0_ref, kk_ref, ka_ref, rk_ref, wl_ref, tri_ref, bd_ref,
                    main_ref, ops_ref, eend_ref, carry_ref, *, blocks_per_row):
    tm = x_ref.shape[0]
    L = RWKV_CHUNK

    @pl.when(pl.program_id(0) == 0)
    def _():
        carry_ref[...] = jnp.zeros_like(carry_ref)

    x = x_ref[...]
    h = (x * g_ref[...]).astype(BF16)
    rs = lax.rsqrt(jnp.mean(x * x, axis=-1, keepdims=True) + NORM_EPS)
    hr = _dot(h, w_ref[:, SRC_RWKV:SRC_RWKV + RWKV_COLS]) * rs
    bd = bd_ref[...]
    tri = tri_ref[...]
    prev_row = jnp.where(pl.program_id(0) % blocks_per_row == 0, 0.0, carry_ref[0:1, :])
    carry_ref[0:1, :] = hr[tm - 1:tm]

    def put(rows, slot, val):
        ops_ref[rows, slot * WIDTH:(slot + 1) * WIDTH] = val.astype(BF16)

    pieces = iter(_main_pieces())

    def emit_pieces(count):
        for _ in range(count):
            piece = next(pieces, None)
            if piece is not None:
                src, dst = piece
                main_ref[:, dst:dst + PIECE] = (_dot(h, w_ref[:, src:src + PIECE]) * rs).astype(main_ref.dtype)

    groups = tm // PREP_ROWS
    per_stage = -(-len(_main_pieces()) // (2 * groups))
    for gi in range(groups):
        r0 = gi * PREP_ROWS
        rows = slice(r0, r0 + PREP_ROWS)
        hb = hr[rows]
        row = lax.broadcasted_iota(jnp.int32, hb.shape, 0)
        prev = jnp.where(row == 0, prev_row, pltpu.roll(hb, 1, 0))
        prev_row = hb[PREP_ROWS - 1:PREP_ROWS]
        z = hb + mu_ref[...] * (prev - hb)
        r, k, v, lo = z[:, 0:WIDTH], z[:, WIDTH:2 * WIDTH], z[:, 2 * WIDTH:3 * WIDTH], z[:, 3 * WIDTH:]
        lane_lo = lax.broadcasted_iota(jnp.int32, lo.shape, 1)
        act = jnp.where(lane_lo < HEAD, jnp.tanh(lo), jnp.where(lane_lo < 2 * HEAD, lo, jax.nn.sigmoid(lo)))
        kkr = k * kk_ref[...]
        lora = _dot(act.astype(BF16), wl_ref[...])
        kk_norm = _seg_sum(kkr * kkr, bd, split=False)
        emit_pieces(per_stage)
        wx = -(w0_ref[...] + lora[:, 0:WIDTH])
        softplus = jnp.maximum(wx, 0.0) + jnp.log(1.0 + jnp.exp(-jnp.abs(wx)))
        lw = -jnp.exp(-softplus - 0.5)
        a = jax.nn.sigmoid(a0_ref[...] + lora[:, WIDTH:2 * WIDTH])
        kk = kkr / jnp.maximum(jnp.sqrt(kk_norm), 1e-12)
        k2 = k * (1.0 + (a - 1.0) * ka_ref[...])
        lw_hi = lw.astype(BF16)
        lw_lo = (lw - lw_hi.astype(F32)).astype(BF16)
        chunks = [slice(c0, c0 + L) for c0 in range(0, PREP_ROWS, L)]
        cums = [_dot(tri, lw_hi[cs]) + _dot(tri, lw_lo[cs]) for cs in chunks]
        bonus = _seg_sum(r * k2 * rk_ref[...], bd) * v
        emit_pieces(per_stage)
        put(rows, OPS_V, v)
        put(rows, OPS_G, lora[:, 2 * WIDTH:3 * WIDTH])
        put(rows, OPS_BONUS, bonus)
        for cs, cum in zip(chunks, cums):
            crow = slice(r0 + cs.start, r0 + cs.stop)
            e_pos = jnp.exp(cum)
            e_neg = jnp.exp(-cum)
            e_end = e_pos[L - 1:L]
            b_t = kk[cs] * a[cs] * e_neg
            k_t = k2[cs] * e_neg
            put(crow, OPS_A, -kk[cs] * jnp.exp(cum - lw[cs]))
            put(crow, OPS_R, r[cs] * e_pos)
            put(crow, OPS_B, b_t)
            put(crow, OPS_K, k_t)
            put(crow, OPS_BH, b_t * e_end)
            put(crow, OPS_KH, k_t * e_end)
            eend_ref[crow.start // L:crow.start // L + 1, :] = e_end
    emit_pieces(len(_main_pieces()))


def _in_proj(x, g, w, layer, prm, tri, bd, tm, blocks_per_row):
    n, d = x.shape
    row = lambda width: _resident((1, width))
    return pl.pallas_call(
        functools.partial(_in_proj_kernel, blocks_per_row=blocks_per_row),
        grid=(n // tm,),
        in_specs=[pl.BlockSpec((tm, d), lambda i: (i, 0)), row(d), _resident(w.shape[1:], layer),
                  row(RWKV_COLS), row(WIDTH), row(WIDTH), row(WIDTH), row(WIDTH), row(WIDTH),
                  _resident((2 * LANES, 3 * WIDTH)), _resident((RWKV_CHUNK, RWKV_CHUNK)), _resident((LANES, LANES))],
        out_specs=[pl.BlockSpec((tm, MAIN_WIDTH), lambda i: (i, 0)),
                   pl.BlockSpec((tm, OPS_WIDTH), lambda i: (i, 0)),
                   pl.BlockSpec((tm // RWKV_CHUNK, WIDTH), lambda i: (i, 0))],
        out_shape=[jax.ShapeDtypeStruct((n, MAIN_WIDTH), BF16),
                   jax.ShapeDtypeStruct((n, OPS_WIDTH), BF16),
                   jax.ShapeDtypeStruct((n // RWKV_CHUNK, WIDTH), F32)],
        scratch_shapes=[pltpu.VMEM((8, RWKV_COLS), F32)],
        compiler_params=_params("arbitrary"),
        name="in_proj",
    )(x, g, w, *prm, tri, bd)


def _attn_kernel(sink_ref, q_ref, kc_ref, vc_ref, kp_ref, vp_ref, gq_ref, gk_ref, bd_ref, o_ref, *, tq):
    i = pl.program_id(1)
    bd = bd_ref[...]
    q = q_ref[0].astype(F32)
    qn = q * lax.rsqrt(_seg_sum(q * q, bd, split=False) * (1.0 / HEAD) + NORM_EPS) * (gq_ref[...] * HEAD ** -0.5)
    k = jnp.concatenate([kp_ref[0], kc_ref[0]], axis=0).astype(F32)
    kn = k * lax.rsqrt(_seg_sum(k * k, bd, split=False) * (1.0 / HEAD) + NORM_EPS) * gk_ref[...]
    v = jnp.concatenate([vp_ref[0], vc_ref[0]], axis=0).astype(F32)

    lane_k = lax.broadcasted_iota(jnp.int32, k.shape, 1)
    lane_q = lax.broadcasted_iota(jnp.int32, (ATTN_BLOCK, LANES), 1)
    low_q = lane_q < HEAD
    key_row = lax.broadcasted_iota(jnp.int32, (ATTN_BLOCK, 2 * ATTN_BLOCK), 0)
    query_col = lax.broadcasted_iota(jnp.int32, (ATTN_BLOCK, 2 * ATTN_BLOCK), 1)
    from_prev = key_row > (query_col & (ATTN_BLOCK - 1))
    first_head = lax.broadcasted_iota(jnp.int32, (1, 2 * ATTN_BLOCK), 1) < ATTN_BLOCK

    k_dup, v_lo, v_hi = [], [], []
    for e in range(2):
        in_e = (lane_k >= e * HEAD) & (lane_k < (e + 1) * HEAD)
        k_e = jnp.where(in_e, kn, 0.0)
        k_dup.append((k_e + pltpu.roll(k_e, HEAD, 1)).astype(BF16))
        v_e = jnp.where(in_e, v, 0.0)
        lo = v_e if e == 0 else pltpu.roll(v_e, HEAD, 1)
        v_lo.append(lo.astype(BF16))
        v_hi.append(pltpu.roll(lo, HEAD, 1).astype(BF16))

    nb = tq // ATTN_BLOCK
    tiles = [(n, c) for n in range(nb) for c in range(N_PAIRS)]

    def keys(arr, n, c):
        return arr[c // 2][n * ATTN_BLOCK:(n + 2) * ATTN_BLOCK]

    qs = []
    for n, c in tiles:
        qc = qn[n * ATTN_BLOCK:(n + 1) * ATTN_BLOCK, c * LANES:(c + 1) * LANES]
        qs.append(jnp.concatenate([jnp.where(low_q, qc, 0.0), jnp.where(low_q, 0.0, qc)], axis=0).astype(BF16))
    s = [_dot(keys(k_dup, n, c), q_, NT) for q_, (n, c) in zip(qs, tiles)]
    no_prev = jnp.where(i == 0, -jnp.inf, 0.0)
    s = [jnp.where(from_prev, s_[:ATTN_BLOCK] + no_prev if n == 0 else s_[:ATTN_BLOCK], s_[ATTN_BLOCK:])
         for s_, (n, c) in zip(s, tiles)]
    sink = [jnp.where(first_head, sink_ref[2 * c], sink_ref[2 * c + 1]) for n, c in tiles]
    m = [jnp.maximum(jnp.max(s_, axis=0, keepdims=True), k_) for s_, k_ in zip(s, sink)]
    p = [jnp.exp(s_ - m_) for s_, m_ in zip(s, m)]
    den = [jnp.sum(p_, axis=0, keepdims=True) + jnp.exp(k_ - m_) for p_, k_, m_ in zip(p, sink, m)]
    p = [(p_ * (1.0 / d_)).astype(BF16) for p_, d_ in zip(p, den)]
    zero = jnp.zeros_like(p[0])
    pb = [jnp.concatenate([jnp.where(from_prev, p_, zero), jnp.where(from_prev, zero, p_)], axis=0) for p_ in p]
    for p_, (n, c) in zip(pb, tiles):
        o = _dot(p_[:, :ATTN_BLOCK], keys(v_lo, n, c), TN) + _dot(p_[:, ATTN_BLOCK:], keys(v_hi, n, c), TN)
        o_ref[0, n * ATTN_BLOCK:(n + 1) * ATTN_BLOCK, c * LANES:(c + 1) * LANES] = o.astype(o_ref.dtype)


def _attention(proj, sinks, gq, gk, bd, tq):
    b, t, _ = proj.shape
    nb = tq // ATTN_BLOCK
    kernel = functools.partial(_attn_kernel, tq=tq)
    return pl.pallas_call(
        kernel,
        grid=(b, t // tq),
        in_specs=[
            pl.BlockSpec(memory_space=pltpu.SMEM),
            pl.BlockSpec((1, tq, WIDTH), lambda bi, i: (bi, i, COL_ATTN_Q // WIDTH)),
            pl.BlockSpec((1, tq, LANES), lambda bi, i: (bi, i, COL_ATTN_K // LANES)),
            pl.BlockSpec((1, tq, LANES), lambda bi, i: (bi, i, COL_ATTN_V // LANES)),
            pl.BlockSpec((1, ATTN_BLOCK, LANES),
                         lambda bi, i: (bi, jnp.maximum(i * nb - 1, 0), COL_ATTN_K // LANES)),
            pl.BlockSpec((1, ATTN_BLOCK, LANES),
                         lambda bi, i: (bi, jnp.maximum(i * nb - 1, 0), COL_ATTN_V // LANES)),
            pl.BlockSpec((1, WIDTH), lambda bi, i: (0, 0)),
            pl.BlockSpec((1, LANES), lambda bi, i: (0, 0)),
            pl.BlockSpec((LANES, LANES), lambda bi, i: (0, 0)),
        ],
        out_specs=pl.BlockSpec((1, tq, WIDTH), lambda bi, i: (bi, i, 0)),
        out_shape=jax.ShapeDtypeStruct((b, t, WIDTH), BF16),
        compiler_params=_params("arbitrary", "arbitrary"),
        name="swa_attention",
    )(sinks, proj, proj, proj, proj, proj, gq, gk, bd)


def _ret_kernel(x_ref, cos_ref, sin_ref, dmat_ref, xi_ref, zeta_ref, cd_ref, bd_ref, o_ref, state_ref, acc_ref):
    @pl.when(pl.program_id(1) == 0)
    def _():
        state_ref[...] = jnp.zeros_like(state_ref)

    c = RET_CHUNK
    nc = x_ref.shape[1] // c
    lane = lax.broadcasted_iota(jnp.int32, (c, LANES), 1)
    first_half = (lane & (HEAD - 1)) < HEAD // 2
    low = lane < HEAD
    rr = lax.broadcasted_iota(jnp.int32, (LANES, LANES), 0)
    cc = lax.broadcasted_iota(jnp.int32, (LANES, LANES), 1)
    same_head = (rr < HEAD) == (cc < HEAD)
    tiles = [(ci, p) for ci in range(nc) for p in range(N_PAIRS)]

    def rotary(col, ci, p):
        rows = slice(ci * c, (ci + 1) * c)
        x = x_ref[0, rows, col + p * LANES:col + (p + 1) * LANES].astype(F32)
        swapped = jnp.where(first_half, pltpu.roll(x, LANES - HEAD // 2, 1), pltpu.roll(x, HEAD // 2, 1))
        sl = slice(p * LANES, (p + 1) * LANES)
        return x * cos_ref[rows, sl] + swapped * sin_ref[rows, sl]

    def split_heads(x):
        zero = jnp.zeros_like(x)
        return jnp.concatenate([jnp.where(low, x, zero), jnp.where(low, zero, x)], axis=0)

    q = [rotary(0, ci, p) for ci, p in tiles]
    k = [rotary(WIDTH, ci, p) * HEAD ** -0.5 for ci, p in tiles]
    v = [x_ref[0, ci * c:(ci + 1) * c, 2 * WIDTH + p * LANES:2 * WIDTH + (p + 1) * LANES] for ci, p in tiles]
    s = [_dot(split_heads(q_.astype(BF16)), k_.astype(BF16), NT) * dmat_ref[p]
         for q_, k_, (ci, p) in zip(q, k, tiles)]
    inner = [_dot(jnp.concatenate([s_[:c], s_[c:]], axis=1).astype(BF16), split_heads(v_)) for s_, v_ in zip(s, v)]
    kv = [jnp.where(same_head, _dot((k_ * zeta_ref[:, p * LANES:(p + 1) * LANES]).astype(BF16), v_, TN), 0.0)
          for k_, v_, (ci, p) in zip(k, v, tiles)]
    entering = []
    state = [state_ref[p] for p in range(N_PAIRS)]
    for ci in range(nc):
        for p in range(N_PAIRS):
            entering.append(state[p])
            state[p] = state[p] * cd_ref[p] + kv[ci * N_PAIRS + p]
    for p in range(N_PAIRS):
        state_ref[p] = state[p]
    for n, (ci, p) in enumerate(tiles):
        cross = _dot((q[n] * xi_ref[:, p * LANES:(p + 1) * LANES]).astype(BF16), entering[n].astype(BF16))
        acc_ref[ci * c:(ci + 1) * c, p * LANES:(p + 1) * LANES] = inner[n] + cross
    o = acc_ref[...]
    o = o * lax.rsqrt(_seg_sum(o * o, bd_ref[...], split=False) * (1.0 / HEAD) + NORM_EPS)
    g = x_ref[0, :, 3 * WIDTH:4 * WIDTH].astype(F32)
    o_ref[0] = (o * (g * jax.nn.sigmoid(g))).astype(o_ref.dtype)


def _retention(proj, tables, bd):
    b, t, _ = proj.shape
    c = RET_CHUNK
    tb = RET_BLOCK
    cos, sin, dmat, xi, zeta, cd = tables
    return pl.pallas_call(
        _ret_kernel,
        grid=(b, t // tb),
        in_specs=[
            pl.BlockSpec((1, tb, 4 * WIDTH), lambda bi, n: (bi, n, COL_RET // (4 * WIDTH))),
            pl.BlockSpec((tb, WIDTH), lambda bi, n: (n, 0)),
            pl.BlockSpec((tb, WIDTH), lambda bi, n: (n, 0)),
            pl.BlockSpec((N_PAIRS, 2 * c, c), lambda bi, n: (0, 0, 0)),
            pl.BlockSpec((c, WIDTH), lambda bi, n: (0, 0)),
            pl.BlockSpec((c, WIDTH), lambda bi, n: (0, 0)),
            pl.BlockSpec((N_PAIRS, LANES, LANES), lambda bi, n: (0, 0, 0)),
            pl.BlockSpec((LANES, LANES), lambda bi, n: (0, 0)),
        ],
        out_specs=pl.BlockSpec((1, tb, WIDTH), lambda bi, n: (bi, n, 0)),
        out_shape=jax.ShapeDtypeStruct((b, t, WIDTH), BF16),
        scratch_shapes=[pltpu.VMEM((N_PAIRS, LANES, LANES), F32), pltpu.VMEM((tb, WIDTH), F32)],
        compiler_params=_params("arbitrary", "arbitrary"),
        name="retention",
    )(proj, cos, sin, dmat, xi, zeta, cd, bd)


def _retention_tables(t):
    c = RET_CHUNK
    half = HEAD // 2
    inv_freq = 1.0 / (ROPE_BASE ** (jnp.arange(half, dtype=F32) * 2.0 / HEAD))
    ang = jnp.arange(t, dtype=F32)[:, None] * inv_freq[None, :]
    cos = jnp.tile(jnp.concatenate([jnp.cos(ang), jnp.cos(ang)], axis=1), (1, N_HEADS))
    sin = jnp.tile(jnp.concatenate([-jnp.sin(ang), jnp.sin(ang)], axis=1), (1, N_HEADS))
    log_gamma = jnp.log1p(-jnp.power(2.0, -5.0 - jnp.arange(N_HEADS, dtype=F32)))
    idx = jnp.arange(c, dtype=F32)
    diff = idx[:, None] - idx[None, :]
    dmat = jnp.where(diff >= 0, jnp.exp(log_gamma[:, None, None] * jnp.maximum(diff, 0.0)), 0.0)
    xi = jnp.exp(log_gamma[:, None] * (idx[None, :] + 1.0))
    zeta = jnp.exp(log_gamma[:, None] * (c - 1.0 - idx[None, :]))
    chunk_decay = jnp.exp(log_gamma * c)
    xi = jnp.repeat(xi.T, HEAD, axis=1)
    zeta = jnp.repeat(zeta.T, HEAD, axis=1)
    cd = jnp.broadcast_to(jnp.repeat(chunk_decay, HEAD).reshape(N_PAIRS, LANES, 1), (N_PAIRS, LANES, LANES))
    dmat = dmat.reshape(N_PAIRS, 2 * c, c)
    return cos, sin, dmat, xi, zeta, cd


def _bdot(a, b, dims=NN):
    return _dot(a.astype(BF16), b.astype(BF16), dims)


def _rwkv_kernel(ops_ref, eend_ref, lng_ref, lnb_ref, bd_ref, o_ref, state_ref, y_ref):
    L = RWKV_CHUNK
    tb = ops_ref.shape[1]
    nc = tb // L

    @pl.when(pl.program_id(1) == 0)
    def _():
        state_ref[...] = jnp.zeros_like(state_ref)

    bd = bd_ref[...]
    lane = lax.broadcasted_iota(jnp.int32, (L, LANES), 1)
    trow = lax.broadcasted_iota(jnp.int32, (L, LANES), 0)
    low = lane < HEAD
    tcol = lane & (HEAD - 1)
    strict = tcol < trow
    incl = tcol <= trow
    eye = (tcol == trow).astype(F32)
    rr = lax.broadcasted_iota(jnp.int32, (LANES, LANES), 0)
    cc = lax.broadcasted_iota(jnp.int32, (LANES, LANES), 1)
    same_head = (rr < HEAD) == (cc < HEAD)
    diag = rr == cc

    def bdiag(z):
        z = z.astype(BF16)
        zero = jnp.zeros_like(z)
        return jnp.concatenate([jnp.where(low, z, zero), jnp.where(low, zero, z)], axis=0)

    def bmm(x, z):
        return _dot(x.astype(BF16), bdiag(z))

    def stack(top, bottom):
        return jnp.concatenate([top, bottom], axis=0).astype(BF16)

    solved = {}

    def solve(chunks):
        tiles = [(c, p) for c in chunks for p in range(N_PAIRS)]

        def operand(slot):
            return [ops_ref[0, c * L:(c + 1) * L, slot * WIDTH + p * LANES:slot * WIDTH + (p + 1) * LANES]
                    for c, p in tiles]

        at_, rt_, bt_, kt_, vp_ = operand(OPS_A), operand(OPS_R), operand(OPS_B), operand(OPS_K), operand(OPS_V)
        bh, kh = operand(OPS_BH), operand(OPS_KH)
        tt = [_dot(stack(a_, r_), jnp.concatenate([bdiag(b_), bdiag(k_)], axis=0), NT)
              for a_, r_, b_, k_ in zip(at_, rt_, bt_, kt_)]
        yield
        t_ab = [jnp.where(strict, m[:L, :LANES], 0.0) for m in tt]
        t_rb = [jnp.where(incl, m[L:, :LANES], 0.0) for m in tt]
        t_ak = [jnp.where(strict, m[:L, LANES:], 0.0) for m in tt]
        t_rk = [jnp.where(incl, m[L:, LANES:], 0.0) for m in tt]
        inv = [eye + t for t in t_ab]
        pw = [bmm(t, t) for t in t_ab]
        yield
        for _ in range(4):
            st = [_dot(stack(w, i), bdiag(w)) for w, i in zip(pw, inv)]
            inv = [i + s[L:] for i, s in zip(inv, st)]
            pw = [s[:L] for s in st]
            yield
        inv = [i + bmm(i, w) for i, w in zip(inv, pw)]
        tvk = [_dot(stack(ta, tk), bdiag(v_)) for ta, tk, v_ in zip(t_ak, t_rk, vp_)]
        yield
        lhs = [stack(i, bmm(t, i)) for i, t in zip(inv, t_rb)]
        yield
        ra = [_dot(x, bdiag(a_)) for x, a_ in zip(lhs, at_)]
        ru = [_dot(x, bdiag(s[:L])) for x, s in zip(lhs, tvk)]
        yield
        qe = [r_.astype(F32) + s[L:] for r_, s in zip(rt_, ra)]
        yl = [s[L:] + s2[L:] for s, s2 in zip(ru, tvk)]
        e_end = [eend_ref[0, c, :, p * LANES:(p + 1) * LANES] for c, p in tiles]
        au = [_bdot(jnp.concatenate([s[:L], s2[:L]], axis=1), b_, TN) for s, s2, b_ in zip(ra, ru, bh)]
        pm = [jnp.where(same_head, s[:LANES], 0.0) + jnp.where(diag, e, 0.0) for s, e in zip(au, e_end)]
        gm = [jnp.where(same_head, s[LANES:] + _bdot(v_, k_, TN), 0.0) for s, v_, k_ in zip(au, vp_, kh)]
        for n, (c, p) in enumerate(tiles):
            solved.setdefault(c, {})[p] = (qe[n], yl[n], pm[n], gm[n])
        yield

    state = [state_ref[p] for p in range(N_PAIRS)]

    def scan_step(c):
        for p in range(N_PAIRS):
            qe, yl, pm, gm = solved[c][p]
            y_ref[c * L:(c + 1) * L, p * LANES:(p + 1) * LANES] = _bdot(qe, state[p], NT) + yl
            state[p] = _bdot(state[p], pm) + gm

    def finish(c):
        rows = slice(c * L, (c + 1) * L)
        y = y_ref[rows]
        mean = _seg_sum(y, bd) * (1.0 / HEAD)
        yield
        d = y - mean
        var = _seg_sum(d * d, bd, split=False) * (1.0 / HEAD)
        yield
        yn = d * lax.rsqrt(var + RWKV_LN_EPS) * lng_ref[...] + lnb_ref[...]
        bonus = ops_ref[0, rows, OPS_BONUS * WIDTH:(OPS_BONUS + 1) * WIDTH].astype(F32)
        g = ops_ref[0, rows, OPS_G * WIDTH:(OPS_G + 1) * WIDTH].astype(F32)
        o_ref[0, rows] = ((yn + bonus) * g).astype(o_ref.dtype)

    active = []

    def tick(new=None):
        if new is not None:
            active.append(new)
        for work in list(active):
            if next(work, "done") == "done":
                active.remove(work)

    half = nc // 2
    for _ in solve(range(half)):
        pass
    pending = list(range(half))
    for stage, _ in enumerate(solve(range(half, nc))):
        if pending and stage % 2 == 1:
            scan_step(pending.pop(0))
    for c in pending:
        scan_step(c)
    for j, c in enumerate(range(half, nc)):
        scan_step(c)
        tick(finish(j))
    for p in range(N_PAIRS):
        state_ref[p] = state[p]
    while active:
        tick()
    active.extend(finish(c) for c in range(half, nc))
    while active:
        tick()


def _rwkv(ops, eend, lnx_g, lnx_b, bd):
    b, t, _ = ops.shape
    tb = RWKV_BLOCK
    nc = tb // RWKV_CHUNK
    row = pl.BlockSpec((1, WIDTH), lambda bi, n: (0, 0))
    return pl.pallas_call(
        _rwkv_kernel,
        grid=(b, t // tb),
        in_specs=[
            pl.BlockSpec((1, tb, OPS_WIDTH), lambda bi, n: (bi, n, 0)),
            pl.BlockSpec((1, nc, 1, WIDTH), lambda bi, n: (bi, n, 0, 0)),
            row, row,
            pl.BlockSpec((LANES, LANES), lambda bi, n: (0, 0)),
        ],
        out_specs=pl.BlockSpec((1, tb, WIDTH), lambda bi, n: (bi, n, 0)),
        out_shape=jax.ShapeDtypeStruct((b, t, WIDTH), BF16),
        scratch_shapes=[
            pltpu.VMEM((N_PAIRS, LANES, LANES), F32),
            pltpu.VMEM((tb, WIDTH), F32),
        ],
        compiler_params=_params("arbitrary", "arbitrary"),
        name="rwkv7",
    )(ops, eend, lnx_g, lnx_b, bd)


def _merge_kernel(oa_ref, ob_ref, oc_ref, g0_ref, g1_ref, g2_ref, x_ref, wa_ref, wb_ref, wc_ref, wo_ref, o_ref):
    def branch(o, w, g):
        return jax.nn.sigmoid(g[...].astype(F32)) * _dot(o[...], w[...])

    mixed = branch(oa_ref, wa_ref, g0_ref) + branch(ob_ref, wb_ref, g1_ref) + branch(oc_ref, wc_ref, g2_ref)
    o_ref[...] = x_ref[...] + _dot(mixed.astype(BF16), wo_ref[...])


def _merge(x, proj, oa, ob, oc, wa, wb, wc, wo, layer, tm):
    n, d = x.shape
    branch = pl.BlockSpec((tm, WIDTH), lambda i: (i, 0))
    gate = lambda j: pl.BlockSpec((tm, d), lambda i: (i, COL_GATES // d + j))
    return pl.pallas_call(
        _merge_kernel,
        grid=(n // tm,),
        in_specs=[branch, branch, branch, gate(0), gate(1), gate(2),
                  pl.BlockSpec((tm, d), lambda i: (i, 0)),
                  _resident((WIDTH, d), layer), _resident((WIDTH, d), layer), _resident((WIDTH, d), layer),
                  _resident((d, d), layer)],
        out_specs=pl.BlockSpec((tm, d), lambda i: (i, 0)),
        out_shape=jax.ShapeDtypeStruct((n, d), F32),
        compiler_params=_params("arbitrary"),
        name="merge_out_proj",
    )(oa, ob, oc, proj, proj, proj, x, wa, wb, wc, wo)


def _ffn_kernel(x_ref, g_ref, wg_ref, wu_ref, wd_ref, o_ref):
    x = x_ref[...]
    h = (x * g_ref[...]).astype(BF16)
    rs = lax.rsqrt(jnp.mean(x * x, axis=-1, keepdims=True) + NORM_EPS)
    gate = _dot(h, wg_ref[...]) * rs
    up = _dot(h, wu_ref[...]) * rs
    act = (gate * jax.nn.sigmoid(gate) * up).astype(BF16)
    o_ref[...] = x + _dot(act, wd_ref[...])


def _ffn(x, g, wg, wu, wd, layer, tm):
    n, d = x.shape
    f = wg.shape[2]
    return pl.pallas_call(
        _ffn_kernel,
        grid=(n // tm,),
        in_specs=[pl.BlockSpec((tm, d), lambda i: (i, 0)), _resident((1, d)),
                  _resident((d, f), layer), _resident((d, f), layer), _resident((f, d), layer)],
        out_specs=pl.BlockSpec((tm, d), lambda i: (i, 0)),
        out_shape=jax.ShapeDtypeStruct((n, d), F32),
        compiler_params=_params("arbitrary"),
        name="swiglu_ffn",
    )(x, g, wg, wu, wd)


def _lora_weight(w2, a2, g2):
    w = jnp.zeros((2 * LANES, 3 * WIDTH), F32)
    w = w.at[0:64, 0:WIDTH].set(w2)
    w = w.at[64:128, WIDTH:2 * WIDTH].set(a2)
    w = w.at[128:256, 2 * WIDTH:3 * WIDTH].set(g2)
    return w.astype(BF16)


def _layer(x, consts, layer, big, norm1_g, attn_q_norm_g, attn_k_norm_g, attn_sinks,
           rwkv_shift_mu, rwkv_w0, rwkv_w2, rwkv_a0, rwkv_a2, rwkv_g2, rwkv_k_k, rwkv_k_a,
           rwkv_r_k, rwkv_lnx_g, rwkv_lnx_b, norm2_g):
    b, t, d = x.shape
    n = b * t
    bd, tri, ret_tables = consts
    w_in, w_attn_o, w_rwkv_o, w_ret_o, w_out, w_ffn_gate, w_ffn_up, w_ffn_down = big
    x2 = x.reshape(n, d)
    row = lambda a: a.reshape(1, WIDTH)
    prm = (rwkv_shift_mu.reshape(1, -1), row(rwkv_w0), row(rwkv_a0), row(rwkv_k_k), row(rwkv_k_a), row(rwkv_r_k),
           _lora_weight(rwkv_w2, rwkv_a2, rwkv_g2))
    tm = IN_PROJ_ROWS
    proj, ops, eend = _in_proj(x2, norm1_g.reshape(1, d), w_in, layer, prm, tri, bd, tm, t // tm)
    proj3 = proj.reshape(b, t, MAIN_WIDTH)

    o_a = _attention(proj3, attn_sinks, jnp.tile(attn_q_norm_g, N_HEADS).reshape(1, WIDTH),
                     jnp.tile(attn_k_norm_g, 2).reshape(1, LANES), bd, tq=512)

    o_b = _rwkv(ops.reshape(b, t, OPS_WIDTH), eend.reshape(b, t // RWKV_CHUNK, 1, WIDTH),
                row(rwkv_lnx_g), row(rwkv_lnx_b), bd)

    o_c = _retention(proj3, ret_tables, bd)

    x2 = _merge(x2, proj, o_a.reshape(n, WIDTH), o_b.reshape(n, WIDTH), o_c.reshape(n, WIDTH),
                w_attn_o, w_rwkv_o, w_ret_o, w_out, layer, tm=512)
    x2 = _ffn(x2, norm2_g.reshape(1, d), w_ffn_gate, w_ffn_up, w_ffn_down, layer, tm=512)
    return x2.reshape(b, t, d)


def _constants(t):
    head_id = jnp.arange(LANES) // HEAD
    bd = (head_id[:, None] == head_id[None, :]).astype(BF16)
    idx = jnp.arange(RWKV_CHUNK)
    tri = (idx[None, :] <= idx[:, None]).astype(BF16)
    return bd, tri, _retention_tables(t)


def kernel(x, norm1_g, w_in, attn_q_norm_g, attn_k_norm_g, attn_sinks, w_attn_o, rwkv_shift_mu, rwkv_w0, rwkv_w2, rwkv_a0, rwkv_a2, rwkv_g2, rwkv_k_k, rwkv_k_a, rwkv_r_k, rwkv_lnx_g, rwkv_lnx_b, w_rwkv_o, w_ret_o, w_out, norm2_g, w_ffn_gate, w_ffn_up, w_ffn_down):
    consts = _constants(x.shape[1])
    big = tuple(w.astype(BF16) for w in (w_in, w_attn_o, w_rwkv_o, w_ret_o, w_out, w_ffn_gate, w_ffn_up, w_ffn_down))
    small = (norm1_g, attn_q_norm_g, attn_k_norm_g, attn_sinks, rwkv_shift_mu, rwkv_w0, rwkv_w2, rwkv_a0, rwkv_a2,
             rwkv_g2, rwkv_k_k, rwkv_k_a, rwkv_r_k, rwkv_lnx_g, rwkv_lnx_b, norm2_g)
    for layer in range(norm1_g.shape[0]):
        x = _layer(x, consts, layer, big, *(w[layer] for w in small))
    return x
```

```python
import functools

import jax
import jax.numpy as jnp
from jax import lax
from jax.experimental import pallas as pl
from jax.experimental.pallas import tpu as pltpu

F32 = jnp.float32
BF16 = jnp.bfloat16

HEAD = 64
LANES = 128
D_MODEL = 1024
N_HEADS = 8
WIDTH = N_HEADS * HEAD
N_PAIRS = WIDTH // LANES
ATTN_BLOCK = 128
RET_CHUNK = 128
RET_BLOCK = 1024
RWKV_CHUNK = 64
RWKV_BLOCK = 512
RWKV_ROWS = 2
RWKV_GROUP = 4
NORM_EPS = 1e-6
RWKV_LN_EPS = 64e-5
ROPE_BASE = 10000.0
VMEM_LIMIT = 56 * 1024 * 1024

COL_RET = 0
COL_GATES = 2048
COL_ATTN_Q = 5120
COL_ATTN_K = 5632
COL_ATTN_V = 5760
MAIN_WIDTH = 5888
SRC_ATTN = 0
SRC_RWKV = 768
SRC_RET = 2560
RWKV_COLS = 1792
OPS_A, OPS_R, OPS_B, OPS_K, OPS_BH, OPS_KH, OPS_V, OPS_G, OPS_BONUS = range(9)
OPS_WIDTH = 9 * WIDTH

NN = (((1,), (0,)), ((), ()))
NT = (((1,), (1,)), ((), ()))
TN = (((0,), (0,)), ((), ()))


def _dot(a, b, dims=NN, precision=None):
    return lax.dot_general(a, b, dims, precision=precision, preferred_element_type=F32)


def _seg_sum(x, bd, split=True):
    cols = []
    for j in range(x.shape[1] // LANES):
        xc = x[:, j * LANES:(j + 1) * LANES]
        hi = xc.astype(BF16)
        s = _dot(hi, bd)
        if split:
            s = s + _dot((xc - hi.astype(F32)).astype(BF16), bd)
        cols.append(s)
    return cols[0] if len(cols) == 1 else jnp.concatenate(cols, axis=1)


def _params(*sem):
    return pltpu.CompilerParams(dimension_semantics=sem, vmem_limit_bytes=VMEM_LIMIT)


def _resident(shape, layer=None):
    nd = len(shape)
    if layer is None:
        return pl.BlockSpec(shape, lambda *_: (0,) * nd, pipeline_mode=pl.Buffered(1))
    return pl.BlockSpec((None,) + tuple(shape), lambda *_: (layer,) + (0,) * nd, pipeline_mode=pl.Buffered(1))


IN_PROJ_ROWS = 512
PREP_ROWS = 256
PIECE = 256


def _main_pieces():
    pieces = [(SRC_RET + d, d) for d in range(0, COL_ATTN_Q, PIECE)]
    pieces += [(SRC_ATTN + d, COL_ATTN_Q + d) for d in range(0, MAIN_WIDTH - COL_ATTN_Q, PIECE)]
    return pieces


def _in_proj_kernel(x_ref, g_ref, w_ref, mu_ref, w0_ref, a0_ref, kk_ref, ka_ref, rk_ref, wl_ref, tri_ref, bd_ref,
                    main_ref, ops_ref, eend_ref, carry_ref, *, blocks_per_row):
    tm = x_ref.shape[0]
    L = RWKV_CHUNK

    @pl.when(pl.program_id(0) == 0)
    def _():
        carry_ref[...] = jnp.zeros_like(carry_ref)

    x = x_ref[...]
    h = (x * g_ref[...]).astype(BF16)
    rs = lax.rsqrt(jnp.mean(x * x, axis=-1, keepdims=True) + NORM_EPS)
    hr = _dot(h, w_ref[:, SRC_RWKV:SRC_RWKV + RWKV_COLS]) * rs
    bd = bd_ref[...]
    tri = tri_ref[...]
    prev_row = jnp.where(pl.program_id(0) % blocks_per_row == 0, 0.0, carry_ref[0:1, :])
    carry_ref[0:1, :] = hr[tm - 1:tm]

    def put(rows, slot, val):
        ops_ref[rows, slot * WIDTH:(slot + 1) * WIDTH] = val.astype(BF16)

    pieces = iter(_main_pieces())

    def emit_pieces(count):
        for _ in range(count):
            piece = next(pieces, None)
            if piece is not None:
                src, dst = piece
                main_ref[:, dst:dst + PIECE] = (_dot(h, w_ref[:, src:src + PIECE]) * rs).astype(main_ref.dtype)

    groups = tm // PREP_ROWS
    per_stage = -(-len(_main_pieces()) // (2 * groups))
    for gi in range(groups):
        r0 = gi * PREP_ROWS
        rows = slice(r0, r0 + PREP_ROWS)
        hb = hr[rows]
        row = lax.broadcasted_iota(jnp.int32, hb.shape, 0)
        prev = jnp.where(row == 0, prev_row, pltpu.roll(hb, 1, 0))
        prev_row = hb[PREP_ROWS - 1:PREP_ROWS]
        z = hb + mu_ref[...] * (prev - hb)
        r, k, v, lo = z[:, 0:WIDTH], z[:, WIDTH:2 * WIDTH], z[:, 2 * WIDTH:3 * WIDTH], z[:, 3 * WIDTH:]
        lane_lo = lax.broadcasted_iota(jnp.int32, lo.shape, 1)
        act = jnp.where(lane_lo < HEAD, jnp.tanh(lo), jnp.where(lane_lo < 2 * HEAD, lo, jax.nn.sigmoid(lo)))
        kkr = k * kk_ref[...]
        lora = _dot(act.astype(BF16), wl_ref[...])
        kk_norm = _seg_sum(kkr * kkr, bd, split=False)
        emit_pieces(per_stage)
        wx = -(w0_ref[...] + lora[:, 0:WIDTH])
        softplus = jnp.maximum(wx, 0.0) + jnp.log(1.0 + jnp.exp(-jnp.abs(wx)))
        lw = -jnp.exp(-softplus - 0.5)
        a = jax.nn.sigmoid(a0_ref[...] + lora[:, WIDTH:2 * WIDTH])
        kk = kkr / jnp.maximum(jnp.sqrt(kk_norm), 1e-12)
        k2 = k * (1.0 + (a - 1.0) * ka_ref[...])
        lw_hi = lw.astype(BF16)
        lw_lo = (lw - lw_hi.astype(F32)).astype(BF16)
        chunks = [slice(c0, c0 + L) for c0 in range(0, PREP_ROWS, L)]
        cums = [_dot(tri, lw_hi[cs]) + _dot(tri, lw_lo[cs]) for cs in chunks]
        bonus = _seg_sum(r * k2 * rk_ref[...], bd) * v
        emit_pieces(per_stage)
        put(rows, OPS_V, v)
        put(rows, OPS_G, lora[:, 2 * WIDTH:3 * WIDTH])
        put(rows, OPS_BONUS, bonus)
        for cs, cum in zip(chunks, cums):
            crow = slice(r0 + cs.start, r0 + cs.stop)
            e_pos = jnp.exp(cum)
            e_neg = jnp.exp(-cum)
            e_end = e_pos[L - 1:L]
            b_t = kk[cs] * a[cs] * e_neg
            k_t = k2[cs] * e_neg
            put(crow, OPS_A, -kk[cs] * jnp.exp(cum - lw[cs]))
            put(crow, OPS_R, r[cs] * e_pos)
            put(crow, OPS_B, b_t)
            put(crow, OPS_K, k_t)
            put(crow, OPS_BH, b_t * e_end)
            put(crow, OPS_KH, k_t * e_end)
            eend_ref[crow.start // L:crow.start // L + 1, :] = e_end
    emit_pieces(len(_main_pieces()))


def _in_proj(x, g, w, layer, prm, tri, bd, tm, blocks_per_row):
    n, d = x.shape
    row = lambda width: _resident((1, width))
    return pl.pallas_call(
        functools.partial(_in_proj_kernel, blocks_per_row=blocks_per_row),
        grid=(n // tm,),
        in_specs=[pl.BlockSpec((tm, d), lambda i: (i, 0)), row(d), _resident(w.shape[1:], layer),
                  row(RWKV_COLS), row(WIDTH), row(WIDTH), row(WIDTH), row(WIDTH), row(WIDTH),
                  _resident((2 * LANES, 3 * WIDTH)), _resident((RWKV_CHUNK, RWKV_CHUNK)), _resident((LANES, LANES))],
        out_specs=[pl.BlockSpec((tm, MAIN_WIDTH), lambda i: (i, 0)),
                   pl.BlockSpec((tm, OPS_WIDTH), lambda i: (i, 0)),
                   pl.BlockSpec((tm // RWKV_CHUNK, WIDTH), lambda i: (i, 0))],
        out_shape=[jax.ShapeDtypeStruct((n, MAIN_WIDTH), BF16),
                   jax.ShapeDtypeStruct((n, OPS_WIDTH), BF16),
                   jax.ShapeDtypeStruct((n // RWKV_CHUNK, WIDTH), F32)],
        scratch_shapes=[pltpu.VMEM((8, RWKV_COLS), F32)],
        compiler_params=_params("arbitrary"),
        name="in_proj",
    )(x, g, w, *prm, tri, bd)


def _attn_kernel(sink_ref, q_ref, kc_ref, vc_ref, kp_ref, vp_ref, gq_ref, gk_ref, bd_ref, o_ref, *, tq):
    i = pl.program_id(1)
    bd = bd_ref[...]
    q = q_ref[0].astype(F32)
    qn = q * lax.rsqrt(_seg_sum(q * q, bd, split=False) * (1.0 / HEAD) + NORM_EPS) * (gq_ref[...] * HEAD ** -0.5)
    k = jnp.concatenate([kp_ref[0], kc_ref[0]], axis=0).astype(F32)
    kn = k * lax.rsqrt(_seg_sum(k * k, bd, split=False) * (1.0 / HEAD) + NORM_EPS) * gk_ref[...]
    v = jnp.concatenate([vp_ref[0], vc_ref[0]], axis=0).astype(F32)

    lane_k = lax.broadcasted_iota(jnp.int32, k.shape, 1)
    lane_q = lax.broadcasted_iota(jnp.int32, (ATTN_BLOCK, LANES), 1)
    low_q = lane_q < HEAD
    key_row = lax.broadcasted_iota(jnp.int32, (ATTN_BLOCK, 2 * ATTN_BLOCK), 0)
    query_col = lax.broadcasted_iota(jnp.int32, (ATTN_BLOCK, 2 * ATTN_BLOCK), 1)
    from_prev = key_row > (query_col & (ATTN_BLOCK - 1))
    first_head = lax.broadcasted_iota(jnp.int32, (1, 2 * ATTN_BLOCK), 1) < ATTN_BLOCK

    k_dup, v_lo, v_hi = [], [], []
    for e in range(2):
        in_e = (lane_k >= e * HEAD) & (lane_k < (e + 1) * HEAD)
        k_e = jnp.where(in_e, kn, 0.0)
        k_dup.append((k_e + pltpu.roll(k_e, HEAD, 1)).astype(BF16))
        v_e = jnp.where(in_e, v, 0.0)
        lo = v_e if e == 0 else pltpu.roll(v_e, HEAD, 1)
        v_lo.append(lo.astype(BF16))
        v_hi.append(pltpu.roll(lo, HEAD, 1).astype(BF16))

    nb = tq // ATTN_BLOCK
    tiles = [(n, c) for n in range(nb) for c in range(N_PAIRS)]

    def keys(arr, n, c):
        return arr[c // 2][n * ATTN_BLOCK:(n + 2) * ATTN_BLOCK]

    qs = []
    for n, c in tiles:
        qc = qn[n * ATTN_BLOCK:(n + 1) * ATTN_BLOCK, c * LANES:(c + 1) * LANES]
        qs.append(jnp.concatenate([jnp.where(low_q, qc, 0.0), jnp.where(low_q, 0.0, qc)], axis=0).astype(BF16))
    s = [_dot(keys(k_dup, n, c), q_, NT) for q_, (n, c) in zip(qs, tiles)]
    no_prev = jnp.where(i == 0, -jnp.inf, 0.0)
    s = [jnp.where(from_prev, s_[:ATTN_BLOCK] + no_prev if n == 0 else s_[:ATTN_BLOCK], s_[ATTN_BLOCK:])
         for s_, (n, c) in zip(s, tiles)]
    sink = [jnp.where(first_head, sink_ref[2 * c], sink_ref[2 * c + 1]) for n, c in tiles]
    m = [jnp.maximum(jnp.max(s_, axis=0, keepdims=True), k_) for s_, k_ in zip(s, sink)]
    p = [jnp.exp(s_ - m_) for s_, m_ in zip(s, m)]
    den = [jnp.sum(p_, axis=0, keepdims=True) + jnp.exp(k_ - m_) for p_, k_, m_ in zip(p, sink, m)]
    p = [(p_ * (1.0 / d_)).astype(BF16) for p_, d_ in zip(p, den)]
    zero = jnp.zeros_like(p[0])
    pb = [jnp.concatenate([jnp.where(from_prev, p_, zero), jnp.where(from_prev, zero, p_)], axis=0) for p_ in p]
    for p_, (n, c) in zip(pb, tiles):
        o = _dot(p_[:, :ATTN_BLOCK], keys(v_lo, n, c), TN) + _dot(p_[:, ATTN_BLOCK:], keys(v_hi, n, c), TN)
        o_ref[0, n * ATTN_BLOCK:(n + 1) * ATTN_BLOCK, c * LANES:(c + 1) * LANES] = o.astype(o_ref.dtype)


def _attention(proj, sinks, gq, gk, bd, tq):
    b, t, _ = proj.shape
    nb = tq // ATTN_BLOCK
    kernel = functools.partial(_attn_kernel, tq=tq)
    return pl.pallas_call(
        kernel,
        grid=(b, t // tq),
        in_specs=[
            pl.BlockSpec(memory_space=pltpu.SMEM),
            pl.BlockSpec((1, tq, WIDTH), lambda bi, i: (bi, i, COL_ATTN_Q // WIDTH)),
            pl.BlockSpec((1, tq, LANES), lambda bi, i: (bi, i, COL_ATTN_K // LANES)),
            pl.BlockSpec((1, tq, LANES), lambda bi, i: (bi, i, COL_ATTN_V // LANES)),
            pl.BlockSpec((1, ATTN_BLOCK, LANES),
                         lambda bi, i: (bi, jnp.maximum(i * nb - 1, 0), COL_ATTN_K // LANES)),
            pl.BlockSpec((1, ATTN_BLOCK, LANES),
                         lambda bi, i: (bi, jnp.maximum(i * nb - 1, 0), COL_ATTN_V // LANES)),
            pl.BlockSpec((1, WIDTH), lambda bi, i: (0, 0)),
            pl.BlockSpec((1, LANES), lambda bi, i: (0, 0)),
            pl.BlockSpec((LANES, LANES), lambda bi, i: (0, 0)),
        ],
        out_specs=pl.BlockSpec((1, tq, WIDTH), lambda bi, i: (bi, i, 0)),
        out_shape=jax.ShapeDtypeStruct((b, t, WIDTH), BF16),
        compiler_params=_params("arbitrary", "arbitrary"),
        name="swa_attention",
    )(sinks, proj, proj, proj, proj, proj, gq, gk, bd)


def _ret_kernel(x_ref, cos_ref, sin_ref, dmat_ref, xi_ref, zeta_ref, cd_ref, bd_ref, o_ref, state_ref, acc_ref):
    @pl.when(pl.program_id(1) == 0)
    def _():
        state_ref[...] = jnp.zeros_like(state_ref)

    c = RET_CHUNK
    nc = x_ref.shape[1] // c
    lane = lax.broadcasted_iota(jnp.int32, (c, LANES), 1)
    first_half = (lane & (HEAD - 1)) < HEAD // 2
    low = lane < HEAD
    rr = lax.broadcasted_iota(jnp.int32, (LANES, LANES), 0)
    cc = lax.broadcasted_iota(jnp.int32, (LANES, LANES), 1)
    same_head = (rr < HEAD) == (cc < HEAD)
    tiles = [(ci, p) for ci in range(nc) for p in range(N_PAIRS)]

    def rotary(col, ci, p):
        rows = slice(ci * c, (ci + 1) * c)
        x = x_ref[0, rows, col + p * LANES:col + (p + 1) * LANES].astype(F32)
        swapped = jnp.where(first_half, pltpu.roll(x, LANES - HEAD // 2, 1), pltpu.roll(x, HEAD // 2, 1))
        sl = slice(p * LANES, (p + 1) * LANES)
        return x * cos_ref[rows, sl] + swapped * sin_ref[rows, sl]

    def split_heads(x):
        zero = jnp.zeros_like(x)
        return jnp.concatenate([jnp.where(low, x, zero), jnp.where(low, zero, x)], axis=0)

    q = [rotary(0, ci, p) for ci, p in tiles]
    k = [rotary(WIDTH, ci, p) * HEAD ** -0.5 for ci, p in tiles]
    v = [x_ref[0, ci * c:(ci + 1) * c, 2 * WIDTH + p * LANES:2 * WIDTH + (p + 1) * LANES] for ci, p in tiles]
    s = [_dot(split_heads(q_.astype(BF16)), k_.astype(BF16), NT) * dmat_ref[p]
         for q_, k_, (ci, p) in zip(q, k, tiles)]
    inner = [_dot(jnp.concatenate([s_[:c], s_[c:]], axis=1).astype(BF16), split_heads(v_)) for s_, v_ in zip(s, v)]
    kv = [jnp.where(same_head, _dot((k_ * zeta_ref[:, p * LANES:(p + 1) * LANES]).astype(BF16), v_, TN), 0.0)
          for k_, v_, (ci, p) in zip(k, v, tiles)]
    entering = []
    state = [state_ref[p] for p in range(N_PAIRS)]
    for ci in range(nc):
        for p in range(N_PAIRS):
            entering.append(state[p])
            state[p] = state[p] * cd_ref[p] + kv[ci * N_PAIRS + p]
    for p in range(N_PAIRS):
        state_ref[p] = state[p]
    for n, (ci, p) in enumerate(tiles):
        cross = _dot((q[n] * xi_ref[:, p * LANES:(p + 1) * LANES]).astype(BF16), entering[n].astype(BF16))
        acc_ref[ci * c:(ci + 1) * c, p * LANES:(p + 1) * LANES] = inner[n] + cross
    o = acc_ref[...]
    o = o * lax.rsqrt(_seg_sum(o * o, bd_ref[...], split=False) * (1.0 / HEAD) + NORM_EPS)
    g = x_ref[0, :, 3 * WIDTH:4 * WIDTH].astype(F32)
    o_ref[0] = (o * (g * jax.nn.sigmoid(g))).astype(o_ref.dtype)


def _retention(proj, tables, bd):
    b, t, _ = proj.shape
    c = RET_CHUNK
    tb = RET_BLOCK
    cos, sin, dmat, xi, zeta, cd = tables
    return pl.pallas_call(
        _ret_kernel,
        grid=(b, t // tb),
        in_specs=[
            pl.BlockSpec((1, tb, 4 * WIDTH), lambda bi, n: (bi, n, COL_RET // (4 * WIDTH))),
            pl.BlockSpec((tb, WIDTH), lambda bi, n: (n, 0)),
            pl.BlockSpec((tb, WIDTH), lambda bi, n: (n, 0)),
            pl.BlockSpec((N_PAIRS, 2 * c, c), lambda bi, n: (0, 0, 0)),
            pl.BlockSpec((c, WIDTH), lambda bi, n: (0, 0)),
            pl.BlockSpec((c, WIDTH), lambda bi, n: (0, 0)),
            pl.BlockSpec((N_PAIRS, LANES, LANES), lambda bi, n: (0, 0, 0)),
            pl.BlockSpec((LANES, LANES), lambda bi, n: (0, 0)),
        ],
        out_specs=pl.BlockSpec((1, tb, WIDTH), lambda bi, n: (bi, n, 0)),
        out_shape=jax.ShapeDtypeStruct((b, t, WIDTH), BF16),
        scratch_shapes=[pltpu.VMEM((N_PAIRS, LANES, LANES), F32), pltpu.VMEM((tb, WIDTH), F32)],
        compiler_params=_params("arbitrary", "arbitrary"),
        name="retention",
    )(proj, cos, sin, dmat, xi, zeta, cd, bd)


def _retention_tables(t):
    c = RET_CHUNK
    half = HEAD // 2
    inv_freq = 1.0 / (ROPE_BASE ** (jnp.arange(half, dtype=F32) * 2.0 / HEAD))
    ang = jnp.arange(t, dtype=F32)[:, None] * inv_freq[None, :]
    cos = jnp.tile(jnp.concatenate([jnp.cos(ang), jnp.cos(ang)], axis=1), (1, N_HEADS))
    sin = jnp.tile(jnp.concatenate([-jnp.sin(ang), jnp.sin(ang)], axis=1), (1, N_HEADS))
    log_gamma = jnp.log1p(-jnp.power(2.0, -5.0 - jnp.arange(N_HEADS, dtype=F32)))
    idx = jnp.arange(c, dtype=F32)
    diff = idx[:, None] - idx[None, :]
    dmat = jnp.where(diff >= 0, jnp.exp(log_gamma[:, None, None] * jnp.maximum(diff, 0.0)), 0.0)
    xi = jnp.exp(log_gamma[:, None] * (idx[None, :] + 1.0))
    zeta = jnp.exp(log_gamma[:, None] * (c - 1.0 - idx[None, :]))
    chunk_decay = jnp.exp(log_gamma * c)
    xi = jnp.repeat(xi.T, HEAD, axis=1)
    zeta = jnp.repeat(zeta.T, HEAD, axis=1)
    cd = jnp.broadcast_to(jnp.repeat(chunk_decay, HEAD).reshape(N_PAIRS, LANES, 1), (N_PAIRS, LANES, LANES))
    dmat = dmat.reshape(N_PAIRS, 2 * c, c)
    return cos, sin, dmat, xi, zeta, cd


def _bdot(a, b, dims=NN):
    return _dot(a.astype(BF16), b.astype(BF16), dims)


def _rwkv_kernel(ops_ref, eend_ref, lng_ref, lnb_ref, bd_ref, o_ref, state_ref, y_ref):
    L = RWKV_CHUNK
    nrow, tb = ops_ref.shape[0], ops_ref.shape[1]
    nc = tb // L

    @pl.when(pl.program_id(1) == 0)
    def _():
        state_ref[...] = jnp.zeros_like(state_ref)

    bd = bd_ref[...]
    lane = lax.broadcasted_iota(jnp.int32, (L, LANES), 1)
    trow = lax.broadcasted_iota(jnp.int32, (L, LANES), 0)
    low = lane < HEAD
    tcol = lane & (HEAD - 1)
    strict = tcol < trow
    incl = tcol <= trow
    eye = (tcol == trow).astype(F32)
    rr = lax.broadcasted_iota(jnp.int32, (LANES, LANES), 0)
    cc = lax.broadcasted_iota(jnp.int32, (LANES, LANES), 1)
    same_head = (rr < HEAD) == (cc < HEAD)
    diag = rr == cc

    def bdiag(z):
        z = z.astype(BF16)
        zero = jnp.zeros_like(z)
        return jnp.concatenate([jnp.where(low, z, zero), jnp.where(low, zero, z)], axis=0)

    def bmm(x, z):
        return _dot(x.astype(BF16), bdiag(z))

    def stack(top, bottom):
        return jnp.concatenate([top, bottom], axis=0).astype(BF16)

    solved = {}

    def solve(units):
        tiles = [(b, c, p) for b, c in units for p in range(N_PAIRS)]

        def operand(slot):
            return [ops_ref[b, c * L:(c + 1) * L, slot * WIDTH + p * LANES:slot * WIDTH + (p + 1) * LANES]
                    for b, c, p in tiles]

        at_, rt_, bt_, kt_, vp_ = operand(OPS_A), operand(OPS_R), operand(OPS_B), operand(OPS_K), operand(OPS_V)
        bh, kh = operand(OPS_BH), operand(OPS_KH)
        tt = [_dot(stack(a_, r_), jnp.concatenate([bdiag(b_), bdiag(k_)], axis=0), NT)
              for a_, r_, b_, k_ in zip(at_, rt_, bt_, kt_)]
        yield
        t_ab = [jnp.where(strict, m[:L, :LANES], 0.0) for m in tt]
        t_rb = [jnp.where(incl, m[L:, :LANES], 0.0) for m in tt]
        t_ak = [jnp.where(strict, m[:L, LANES:], 0.0) for m in tt]
        t_rk = [jnp.where(incl, m[L:, LANES:], 0.0) for m in tt]
        inv = [eye + t for t in t_ab]
        pw = [bmm(t, t) for t in t_ab]
        yield
        for _ in range(4):
            st = [_dot(stack(w, i), bdiag(w)) for w, i in zip(pw, inv)]
            inv = [i + s[L:] for i, s in zip(inv, st)]
            pw = [s[:L] for s in st]
            yield
        inv = [i + bmm(i, w) for i, w in zip(inv, pw)]
        tvk = [_dot(stack(ta, tk), bdiag(v_)) for ta, tk, v_ in zip(t_ak, t_rk, vp_)]
        yield
        lhs = [stack(i, bmm(t, i)) for i, t in zip(inv, t_rb)]
        yield
        ra = [_dot(x, bdiag(a_)) for x, a_ in zip(lhs, at_)]
        ru = [_dot(x, bdiag(s[:L])) for x, s in zip(lhs, tvk)]
        yield
        qe = [r_.astype(F32) + s[L:] for r_, s in zip(rt_, ra)]
        yl = [s[L:] + s2[L:] for s, s2 in zip(ru, tvk)]
        e_end = [eend_ref[b, c, :, p * LANES:(p + 1) * LANES] for b, c, p in tiles]
        au = [_bdot(jnp.concatenate([s[:L], s2[:L]], axis=1), b_, TN) for s, s2, b_ in zip(ra, ru, bh)]
        pm = [jnp.where(same_head, s[:LANES], 0.0) + jnp.where(diag, e, 0.0) for s, e in zip(au, e_end)]
        gm = [jnp.where(same_head, s[LANES:] + _bdot(v_, k_, TN), 0.0) for s, v_, k_ in zip(au, vp_, kh)]
        for n, (b, c, p) in enumerate(tiles):
            solved.setdefault((b, c), {})[p] = (qe[n], yl[n], pm[n], gm[n])
        yield

    state = {(b, p): state_ref[b, p] for b in range(nrow) for p in range(N_PAIRS)}

    def scan_step(unit):
        b, c = unit
        for p in range(N_PAIRS):
            qe, yl, pm, gm = solved[unit][p]
            y_ref[b, c * L:(c + 1) * L, p * LANES:(p + 1) * LANES] = _bdot(qe, state[b, p], NT) + yl
            state[b, p] = _bdot(state[b, p], pm) + gm

    def finish(unit):
        b, c = unit
        rows = slice(c * L, (c + 1) * L)
        y = y_ref[b, rows]
        mean = _seg_sum(y, bd) * (1.0 / HEAD)
        yield
        d = y - mean
        var = _seg_sum(d * d, bd, split=False) * (1.0 / HEAD)
        yield
        yn = d * lax.rsqrt(var + RWKV_LN_EPS) * lng_ref[...] + lnb_ref[...]
        bonus = ops_ref[b, rows, OPS_BONUS * WIDTH:(OPS_BONUS + 1) * WIDTH].astype(F32)
        g = ops_ref[b, rows, OPS_G * WIDTH:(OPS_G + 1) * WIDTH].astype(F32)
        o_ref[b, rows] = ((yn + bonus) * g).astype(o_ref.dtype)

    active = []

    def tick(new=None):
        if new is not None:
            active.append(new)
        for work in list(active):
            if next(work, "done") == "done":
                active.remove(work)

    groups = [[(b, c) for c in range(c0, c0 + RWKV_GROUP)] for b in range(nrow) for c0 in range(0, nc, RWKV_GROUP)]
    scans, fins = [], []
    for gi, group in enumerate(groups):
        for stage, _ in enumerate(solve(group)):
            if stage % 2 == 1:
                if scans:
                    scan_step(scans.pop(0))
                tick(fins.pop(0) if fins else None)
        for unit in scans:
            scan_step(unit)
        scans = list(group)
        if gi > 0:
            fins += [finish(u) for u in groups[gi - 1]]
    for unit in scans:
        scan_step(unit)
        tick(fins.pop(0) if fins else None)
    for b in range(nrow):
        for p in range(N_PAIRS):
            state_ref[b, p] = state[b, p]
    while active or fins:
        tick(fins.pop(0) if fins else None)
    active.extend(finish(u) for u in groups[-1])
    while active:
        tick()


def _rwkv(ops, eend, lnx_g, lnx_b, bd):
    b, t, _ = ops.shape
    tb = RWKV_BLOCK
    nr = RWKV_ROWS
    nc = tb // RWKV_CHUNK
    row = pl.BlockSpec((1, WIDTH), lambda bi, n: (0, 0))
    return pl.pallas_call(
        _rwkv_kernel,
        grid=(b // nr, t // tb),
        in_specs=[
            pl.BlockSpec((nr, tb, OPS_WIDTH), lambda bi, n: (bi, n, 0)),
            pl.BlockSpec((nr, nc, 1, WIDTH), lambda bi, n: (bi, n, 0, 0)),
            row, row,
            pl.BlockSpec((LANES, LANES), lambda bi, n: (0, 0)),
        ],
        out_specs=pl.BlockSpec((nr, tb, WIDTH), lambda bi, n: (bi, n, 0)),
        out_shape=jax.ShapeDtypeStruct((b, t, WIDTH), BF16),
        scratch_shapes=[
            pltpu.VMEM((nr, N_PAIRS, LANES, LANES), F32),
            pltpu.VMEM((nr, tb, WIDTH), F32),
        ],
        compiler_params=_params("arbitrary", "arbitrary"),
        name="rwkv7",
    )(ops, eend, lnx_g, lnx_b, bd)


def _merge_kernel(oa_ref, ob_ref, oc_ref, g0_ref, g1_ref, g2_ref, x_ref, wa_ref, wb_ref, wc_ref, wo_ref, o_ref):
    def branch(o, w, g):
        return jax.nn.sigmoid(g[...].astype(F32)) * _dot(o[...], w[...])

    mixed = branch(oa_ref, wa_ref, g0_ref) + branch(ob_ref, wb_ref, g1_ref) + branch(oc_ref, wc_ref, g2_ref)
    o_ref[...] = x_ref[...] + _dot(mixed.astype(BF16), wo_ref[...])


def _merge(x, proj, oa, ob, oc, wa, wb, wc, wo, layer, tm):
    n, d = x.shape
    branch = pl.BlockSpec((tm, WIDTH), lambda i: (i, 0))
    gate = lambda j: pl.BlockSpec((tm, d), lambda i: (i, COL_GATES // d + j))
    return pl.pallas_call(
        _merge_kernel,
        grid=(n // tm,),
        in_specs=[branch, branch, branch, gate(0), gate(1), gate(2),
                  pl.BlockSpec((tm, d), lambda i: (i, 0)),
                  _resident((WIDTH, d), layer), _resident((WIDTH, d), layer), _resident((WIDTH, d), layer),
                  _resident((d, d), layer)],
        out_specs=pl.BlockSpec((tm, d), lambda i: (i, 0)),
        out_shape=jax.ShapeDtypeStruct((n, d), F32),
        compiler_params=_params("arbitrary"),
        name="merge_out_proj",
    )(oa, ob, oc, proj, proj, proj, x, wa, wb, wc, wo)


def _ffn_kernel(x_ref, g_ref, wg_ref, wu_ref, wd_ref, o_ref):
    x = x_ref[...]
    h = (x * g_ref[...]).astype(BF16)
    rs = lax.rsqrt(jnp.mean(x * x, axis=-1, keepdims=True) + NORM_EPS)
    gate = _dot(h, wg_ref[...]) * rs
    up = _dot(h, wu_ref[...]) * rs
    act = (gate * jax.nn.sigmoid(gate) * up).astype(BF16)
    o_ref[...] = x + _dot(act, wd_ref[...])


def _ffn(x, g, wg, wu, wd, layer, tm):
    n, d = x.shape
    f = wg.shape[2]
    return pl.pallas_call(
        _ffn_kernel,
        grid=(n // tm,),
        in_specs=[pl.BlockSpec((tm, d), lambda i: (i, 0)), _resident((1, d)),
                  _resident((d, f), layer), _resident((d, f), layer), _resident((f, d), layer)],
        out_specs=pl.BlockSpec((tm, d), lambda i: (i, 0)),
        out_shape=jax.ShapeDtypeStruct((n, d), F32),
        compiler_params=_params("arbitrary"),
        name="swiglu_ffn",
    )(x, g, wg, wu, wd)


def _lora_weight(w2, a2, g2):
    w = jnp.zeros((2 * LANES, 3 * WIDTH), F32)
    w = w.at[0:64, 0:WIDTH].set(w2)
    w = w.at[64:128, WIDTH:2 * WIDTH].set(a2)
    w = w.at[128:256, 2 * WIDTH:3 * WIDTH].set(g2)
    return w.astype(BF16)


def _layer(x, consts, layer, big, norm1_g, attn_q_norm_g, attn_k_norm_g, attn_sinks,
           rwkv_shift_mu, rwkv_w0, rwkv_w2, rwkv_a0, rwkv_a2, rwkv_g2, rwkv_k_k, rwkv_k_a,
           rwkv_r_k, rwkv_lnx_g, rwkv_lnx_b, norm2_g):
    b, t, d = x.shape
    n = b * t
    bd, tri, ret_tables = consts
    w_in, w_attn_o, w_rwkv_o, w_ret_o, w_out, w_ffn_gate, w_ffn_up, w_ffn_down = big
    x2 = x.reshape(n, d)
    row = lambda a: a.reshape(1, WIDTH)
    prm = (rwkv_shift_mu.reshape(1, -1), row(rwkv_w0), row(rwkv_a0), row(rwkv_k_k), row(rwkv_k_a), row(rwkv_r_k),
           _lora_weight(rwkv_w2, rwkv_a2, rwkv_g2))
    tm = IN_PROJ_ROWS
    proj, ops, eend = _in_proj(x2, norm1_g.reshape(1, d), w_in, layer, prm, tri, bd, tm, t // tm)
    proj3 = proj.reshape(b, t, MAIN_WIDTH)

    o_a = _attention(proj3, attn_sinks, jnp.tile(attn_q_norm_g, N_HEADS).reshape(1, WIDTH),
                     jnp.tile(attn_k_norm_g, 2).reshape(1, LANES), bd, tq=1024)

    o_b = _rwkv(ops.reshape(b, t, OPS_WIDTH), eend.reshape(b, t // RWKV_CHUNK, 1, WIDTH),
                row(rwkv_lnx_g), row(rwkv_lnx_b), bd)

    o_c = _retention(proj3, ret_tables, bd)

    x2 = _merge(x2, proj, o_a.reshape(n, WIDTH), o_b.reshape(n, WIDTH), o_c.reshape(n, WIDTH),
                w_attn_o, w_rwkv_o, w_ret_o, w_out, layer, tm=1024)
    x2 = _ffn(x2, norm2_g.reshape(1, d), w_ffn_gate, w_ffn_up, w_ffn_down, layer, tm=512)
    return x2.reshape(b, t, d)


def _constants(t):
    head_id = jnp.arange(LANES) // HEAD
    bd = (head_id[:, None] == head_id[None, :]).astype(BF16)
    idx = jnp.arange(RWKV_CHUNK)
    tri = (idx[None, :] <= idx[:, None]).astype(BF16)
    return bd, tri, _retention_tables(t)


def kernel(x, norm1_g, w_in, attn_q_norm_g, attn_k_norm_g, attn_sinks, w_attn_o, rwkv_shift_mu, rwkv_w0, rwkv_w2, rwkv_a0, rwkv_a2, rwkv_g2, rwkv_k_k, rwkv_k_a, rwkv_r_k, rwkv_lnx_g, rwkv_lnx_b, w_rwkv_o, w_ret_o, w_out, norm2_g, w_ffn_gate, w_ffn_up, w_ffn_down):
    consts = _constants(x.shape[1])
    big = tuple(w.astype(BF16) for w in (w_in, w_attn_o, w_rwkv_o, w_ret_o, w_out, w_ffn_gate, w_ffn_up, w_ffn_down))
    small = (norm1_g, attn_q_norm_g, attn_k_norm_g, attn_sinks, rwkv_shift_mu, rwkv_w0, rwkv_w2, rwkv_a0, rwkv_a2,
             rwkv_g2, rwkv_k_k, rwkv_k_a, rwkv_r_k, rwkv_lnx_g, rwkv_lnx_b, norm2_g)
    for layer in range(norm1_g.shape[0]):
        x = _layer(x, consts, layer, big, *(w[layer] for w in small))
    return x
```

```python
import functools

import jax
import jax.numpy as jnp
from jax import lax
from jax.experimental import pallas as pl
from jax.experimental.pallas import tpu as pltpu

F32 = jnp.float32
BF16 = jnp.bfloat16

HEAD = 64
LANES = 128
D_MODEL = 1024
N_HEADS = 8
WIDTH = N_HEADS * HEAD
N_PAIRS = WIDTH // LANES
ATTN_BLOCK = 128
ATTN_ROWS = 1024
MERGE_ROWS = 1024
FFN_ROWS = 512
RET_CHUNK = 128
RET_BLOCK = 1024
RWKV_CHUNK = 64
RWKV_BLOCK = 512
RWKV_ROWS = 2
RWKV_GROUP = 4
NORM_EPS = 1e-6
RWKV_LN_EPS = 64e-5
ROPE_BASE = 10000.0
VMEM_LIMIT = 56 * 1024 * 1024

COL_RET = 0
COL_GATES = 2048
COL_ATTN_Q = 5120
COL_ATTN_K = 5632
COL_ATTN_V = 5760
MAIN_WIDTH = 5888
SRC_ATTN = 0
SRC_RWKV = 768
SRC_RET = 2560
RWKV_COLS = 1792
OPS_A, OPS_R, OPS_B, OPS_K, OPS_BH, OPS_KH, OPS_V, OPS_G, OPS_BONUS = range(9)
OPS_WIDTH = 9 * WIDTH

NN = (((1,), (0,)), ((), ()))
NT = (((1,), (1,)), ((), ()))
TN = (((0,), (0,)), ((), ()))


def _dot(a, b, dims=NN, precision=None):
    return lax.dot_general(a, b, dims, precision=precision, preferred_element_type=F32)


def _seg_sum(x, bd, split=True):
    cols = []
    for j in range(x.shape[1] // LANES):
        xc = x[:, j * LANES:(j + 1) * LANES]
        hi = xc.astype(BF16)
        s = _dot(hi, bd)
        if split:
            s = s + _dot((xc - hi.astype(F32)).astype(BF16), bd)
        cols.append(s)
    return cols[0] if len(cols) == 1 else jnp.concatenate(cols, axis=1)


def _params(*sem):
    return pltpu.CompilerParams(dimension_semantics=sem, vmem_limit_bytes=VMEM_LIMIT)


def _resident(shape, layer=None):
    nd = len(shape)
    if layer is None:
        return pl.BlockSpec(shape, lambda *_: (0,) * nd, pipeline_mode=pl.Buffered(1))
    return pl.BlockSpec((None,) + tuple(shape), lambda *_: (layer,) + (0,) * nd, pipeline_mode=pl.Buffered(1))


IN_PROJ_ROWS = 512
PREP_ROWS = 256
PIECE = 256
LEAD_PIECES = 4


def _main_pieces():
    pieces = [(SRC_RET + d, d) for d in range(0, COL_ATTN_Q, PIECE)]
    pieces += [(SRC_ATTN + d, COL_ATTN_Q + d) for d in range(0, MAIN_WIDTH - COL_ATTN_Q, PIECE)]
    return pieces


def _in_proj_kernel(x_ref, g_ref, w_ref, mu_ref, w0_ref, a0_ref, kk_ref, ka_ref, rk_ref, wl_ref, tri_ref, bd_ref,
                    main_ref, ops_ref, eend_ref, carry_ref, *, blocks_per_row):
    tm = x_ref.shape[0]
    L = RWKV_CHUNK

    @pl.when(pl.program_id(0) == 0)
    def _():
        carry_ref[...] = jnp.zeros_like(carry_ref)

    x = x_ref[...]
    h = (x * g_ref[...]).astype(BF16)
    rs = lax.rsqrt(jnp.mean(x * x, axis=-1, keepdims=True) + NORM_EPS)
    hr = _dot(h, w_ref[:, SRC_RWKV:SRC_RWKV + RWKV_COLS]) * rs
    bd = bd_ref[...]
    tri = tri_ref[...]
    prev_row = jnp.where(pl.program_id(0) % blocks_per_row == 0, 0.0, carry_ref[0:1, :])
    carry_ref[0:1, :] = hr[tm - 1:tm]

    def put(rows, slot, val):
        ops_ref[rows, slot * WIDTH:(slot + 1) * WIDTH] = val.astype(BF16)

    pieces = iter(_main_pieces())

    def emit_pieces(count):
        for _ in range(count):
            piece = next(pieces, None)
            if piece is not None:
                src, dst = piece
                main_ref[:, dst:dst + PIECE] = (_dot(h, w_ref[:, src:src + PIECE]) * rs).astype(main_ref.dtype)

    groups = tm // PREP_ROWS
    emit_pieces(LEAD_PIECES)
    per_stage = -(-(len(_main_pieces()) - LEAD_PIECES) // (2 * groups))
    for gi in range(groups):
        r0 = gi * PREP_ROWS
        rows = slice(r0, r0 + PREP_ROWS)
        hb = hr[rows]
        row = lax.broadcasted_iota(jnp.int32, hb.shape, 0)
        prev = jnp.where(row == 0, prev_row, pltpu.roll(hb, 1, 0))
        prev_row = hb[PREP_ROWS - 1:PREP_ROWS]
        z = hb + mu_ref[...] * (prev - hb)
        r, k, v, lo = z[:, 0:WIDTH], z[:, WIDTH:2 * WIDTH], z[:, 2 * WIDTH:3 * WIDTH], z[:, 3 * WIDTH:]
        lane_lo = lax.broadcasted_iota(jnp.int32, lo.shape, 1)
        act = jnp.where(lane_lo < HEAD, jnp.tanh(lo), jnp.where(lane_lo < 2 * HEAD, lo, jax.nn.sigmoid(lo)))
        kkr = k * kk_ref[...]
        lora = _dot(act.astype(BF16), wl_ref[...])
        kk_norm = _seg_sum(kkr * kkr, bd, split=False)
        emit_pieces(per_stage)
        wx = -(w0_ref[...] + lora[:, 0:WIDTH])
        softplus = jnp.maximum(wx, 0.0) + jnp.log(1.0 + jnp.exp(-jnp.abs(wx)))
        lw = -jnp.exp(-softplus - 0.5)
        a = jax.nn.sigmoid(a0_ref[...] + lora[:, WIDTH:2 * WIDTH])
        kk = kkr / jnp.maximum(jnp.sqrt(kk_norm), 1e-12)
        k2 = k * (1.0 + (a - 1.0) * ka_ref[...])
        lw_hi = lw.astype(BF16)
        lw_lo = (lw - lw_hi.astype(F32)).astype(BF16)
        chunks = [slice(c0, c0 + L) for c0 in range(0, PREP_ROWS, L)]
        cums = [_dot(tri, lw_hi[cs]) + _dot(tri, lw_lo[cs]) for cs in chunks]
        bonus = _seg_sum(r * k2 * rk_ref[...], bd) * v
        emit_pieces(per_stage)
        put(rows, OPS_V, v)
        put(rows, OPS_G, lora[:, 2 * WIDTH:3 * WIDTH])
        put(rows, OPS_BONUS, bonus)
        for cs, cum in zip(chunks, cums):
            crow = slice(r0 + cs.start, r0 + cs.stop)
            e_pos = jnp.exp(cum)
            e_neg = jnp.exp(-cum)
            e_end = e_pos[L - 1:L]
            b_t = kk[cs] * a[cs] * e_neg
            k_t = k2[cs] * e_neg
            put(crow, OPS_A, -kk[cs] * jnp.exp(cum - lw[cs]))
            put(crow, OPS_R, r[cs] * e_pos)
            put(crow, OPS_B, b_t)
            put(crow, OPS_K, k_t)
            put(crow, OPS_BH, b_t * e_end)
            put(crow, OPS_KH, k_t * e_end)
            eend_ref[crow.start // L:crow.start // L + 1, :] = e_end
    emit_pieces(len(_main_pieces()))


def _in_proj(x, g, w, layer, prm, tri, bd, tm, blocks_per_row):
    n, d = x.shape
    row = lambda width: _resident((1, width))
    return pl.pallas_call(
        functools.partial(_in_proj_kernel, blocks_per_row=blocks_per_row),
        grid=(n // tm,),
        in_specs=[pl.BlockSpec((tm, d), lambda i: (i, 0)), row(d), _resident(w.shape[1:], layer),
                  row(RWKV_COLS), row(WIDTH), row(WIDTH), row(WIDTH), row(WIDTH), row(WIDTH),
                  _resident((2 * LANES, 3 * WIDTH)), _resident((RWKV_CHUNK, RWKV_CHUNK)), _resident((LANES, LANES))],
        out_specs=[pl.BlockSpec((tm, MAIN_WIDTH), lambda i: (i, 0)),
                   pl.BlockSpec((tm, OPS_WIDTH), lambda i: (i, 0)),
                   pl.BlockSpec((tm // RWKV_CHUNK, WIDTH), lambda i: (i, 0))],
        out_shape=[jax.ShapeDtypeStruct((n, MAIN_WIDTH), BF16),
                   jax.ShapeDtypeStruct((n, OPS_WIDTH), BF16),
                   jax.ShapeDtypeStruct((n // RWKV_CHUNK, WIDTH), F32)],
        scratch_shapes=[pltpu.VMEM((8, RWKV_COLS), F32)],
        compiler_params=_params("arbitrary"),
        name="in_proj",
    )(x, g, w, *prm, tri, bd)


def _attn_kernel(sink_ref, q_ref, kc_ref, vc_ref, kp_ref, vp_ref, gq_ref, gk_ref, bd_ref, o_ref, *, tq):
    i = pl.program_id(1)
    bd = bd_ref[...]
    q = q_ref[0].astype(F32)
    qn = q * lax.rsqrt(_seg_sum(q * q, bd, split=False) * (1.0 / HEAD) + NORM_EPS) * (gq_ref[...] * HEAD ** -0.5)
    k = jnp.concatenate([kp_ref[0], kc_ref[0]], axis=0).astype(F32)
    kn = k * lax.rsqrt(_seg_sum(k * k, bd, split=False) * (1.0 / HEAD) + NORM_EPS) * gk_ref[...]
    v = jnp.concatenate([vp_ref[0], vc_ref[0]], axis=0).astype(F32)

    lane_k = lax.broadcasted_iota(jnp.int32, k.shape, 1)
    lane_q = lax.broadcasted_iota(jnp.int32, (ATTN_BLOCK, LANES), 1)
    low_q = lane_q < HEAD
    key_row = lax.broadcasted_iota(jnp.int32, (ATTN_BLOCK, 2 * ATTN_BLOCK), 0)
    query_col = lax.broadcasted_iota(jnp.int32, (ATTN_BLOCK, 2 * ATTN_BLOCK), 1)
    from_prev = key_row > (query_col & (ATTN_BLOCK - 1))
    first_head = lax.broadcasted_iota(jnp.int32, (1, 2 * ATTN_BLOCK), 1) < ATTN_BLOCK

    k_dup, v_lo, v_hi = [], [], []
    for e in range(2):
        in_e = (lane_k >= e * HEAD) & (lane_k < (e + 1) * HEAD)
        k_e = jnp.where(in_e, kn, 0.0)
        k_dup.append((k_e + pltpu.roll(k_e, HEAD, 1)).astype(BF16))
        v_e = jnp.where(in_e, v, 0.0)
        lo = v_e if e == 0 else pltpu.roll(v_e, HEAD, 1)
        v_lo.append(lo.astype(BF16))
        v_hi.append(pltpu.roll(lo, HEAD, 1).astype(BF16))

    nb = tq // ATTN_BLOCK
    tiles = [(n, c) for n in range(nb) for c in range(N_PAIRS)]

    def keys(arr, n, c):
        return arr[c // 2][n * ATTN_BLOCK:(n + 2) * ATTN_BLOCK]

    qs = []
    for n, c in tiles:
        qc = qn[n * ATTN_BLOCK:(n + 1) * ATTN_BLOCK, c * LANES:(c + 1) * LANES]
        qs.append(jnp.concatenate([jnp.where(low_q, qc, 0.0), jnp.where(low_q, 0.0, qc)], axis=0).astype(BF16))
    s = [_dot(keys(k_dup, n, c), q_, NT) for q_, (n, c) in zip(qs, tiles)]
    no_prev = jnp.where(i == 0, -jnp.inf, 0.0)
    s = [jnp.where(from_prev, s_[:ATTN_BLOCK] + no_prev if n == 0 else s_[:ATTN_BLOCK], s_[ATTN_BLOCK:])
         for s_, (n, c) in zip(s, tiles)]
    sink = [jnp.where(first_head, sink_ref[2 * c], sink_ref[2 * c + 1]) for n, c in tiles]
    m = [jnp.maximum(jnp.max(s_, axis=0, keepdims=True), k_) for s_, k_ in zip(s, sink)]
    p = [jnp.exp(s_ - m_) for s_, m_ in zip(s, m)]
    den = [jnp.sum(p_, axis=0, keepdims=True) + jnp.exp(k_ - m_) for p_, k_, m_ in zip(p, sink, m)]
    p = [(p_ * (1.0 / d_)).astype(BF16) for p_, d_ in zip(p, den)]
    zero = jnp.zeros_like(p[0])
    pb = [jnp.concatenate([jnp.where(from_prev, p_, zero), jnp.where(from_prev, zero, p_)], axis=0) for p_ in p]
    for p_, (n, c) in zip(pb, tiles):
        o = _dot(p_[:, :ATTN_BLOCK], keys(v_lo, n, c), TN) + _dot(p_[:, ATTN_BLOCK:], keys(v_hi, n, c), TN)
        o_ref[0, n * ATTN_BLOCK:(n + 1) * ATTN_BLOCK, c * LANES:(c + 1) * LANES] = o.astype(o_ref.dtype)


def _attention(proj, sinks, gq, gk, bd, tq):
    b, t, _ = proj.shape
    nb = tq // ATTN_BLOCK
    kernel = functools.partial(_attn_kernel, tq=tq)
    return pl.pallas_call(
        kernel,
        grid=(b, t // tq),
        in_specs=[
            pl.BlockSpec(memory_space=pltpu.SMEM),
            pl.BlockSpec((1, tq, WIDTH), lambda bi, i: (bi, i, COL_ATTN_Q // WIDTH)),
            pl.BlockSpec((1, tq, LANES), lambda bi, i: (bi, i, COL_ATTN_K // LANES)),
            pl.BlockSpec((1, tq, LANES), lambda bi, i: (bi, i, COL_ATTN_V // LANES)),
            pl.BlockSpec((1, ATTN_BLOCK, LANES),
                         lambda bi, i: (bi, jnp.maximum(i * nb - 1, 0), COL_ATTN_K // LANES)),
            pl.BlockSpec((1, ATTN_BLOCK, LANES),
                         lambda bi, i: (bi, jnp.maximum(i * nb - 1, 0), COL_ATTN_V // LANES)),
            pl.BlockSpec((1, WIDTH), lambda bi, i: (0, 0)),
            pl.BlockSpec((1, LANES), lambda bi, i: (0, 0)),
            pl.BlockSpec((LANES, LANES), lambda bi, i: (0, 0)),
        ],
        out_specs=pl.BlockSpec((1, tq, WIDTH), lambda bi, i: (bi, i, 0)),
        out_shape=jax.ShapeDtypeStruct((b, t, WIDTH), BF16),
        compiler_params=_params("arbitrary", "arbitrary"),
        name="swa_attention",
    )(sinks, proj, proj, proj, proj, proj, gq, gk, bd)


def _ret_kernel(x_ref, cos_ref, sin_ref, dmat_ref, xi_ref, zeta_ref, cd_ref, bd_ref, o_ref, state_ref, acc_ref):
    @pl.when(pl.program_id(1) == 0)
    def _():
        state_ref[...] = jnp.zeros_like(state_ref)

    c = RET_CHUNK
    nc = x_ref.shape[1] // c
    lane = lax.broadcasted_iota(jnp.int32, (c, LANES), 1)
    first_half = (lane & (HEAD - 1)) < HEAD // 2
    low = lane < HEAD
    rr = lax.broadcasted_iota(jnp.int32, (LANES, LANES), 0)
    cc = lax.broadcasted_iota(jnp.int32, (LANES, LANES), 1)
    same_head = (rr < HEAD) == (cc < HEAD)
    tiles = [(ci, p) for ci in range(nc) for p in range(N_PAIRS)]

    def rotary(col, ci, p):
        rows = slice(ci * c, (ci + 1) * c)
        x = x_ref[0, rows, col + p * LANES:col + (p + 1) * LANES].astype(F32)
        swapped = jnp.where(first_half, pltpu.roll(x, LANES - HEAD // 2, 1), pltpu.roll(x, HEAD // 2, 1))
        sl = slice(p * LANES, (p + 1) * LANES)
        return x * cos_ref[rows, sl] + swapped * sin_ref[rows, sl]

    def split_heads(x):
        zero = jnp.zeros_like(x)
        return jnp.concatenate([jnp.where(low, x, zero), jnp.where(low, zero, x)], axis=0)

    q = [rotary(0, ci, p) for ci, p in tiles]
    k = [rotary(WIDTH, ci, p) * HEAD ** -0.5 for ci, p in tiles]
    v = [x_ref[0, ci * c:(ci + 1) * c, 2 * WIDTH + p * LANES:2 * WIDTH + (p + 1) * LANES] for ci, p in tiles]
    s = [_dot(split_heads(q_.astype(BF16)), k_.astype(BF16), NT) * dmat_ref[p]
         for q_, k_, (ci, p) in zip(q, k, tiles)]
    inner = [_dot(jnp.concatenate([s_[:c], s_[c:]], axis=1).astype(BF16), split_heads(v_)) for s_, v_ in zip(s, v)]
    kv = [jnp.where(same_head, _dot((k_ * zeta_ref[:, p * LANES:(p + 1) * LANES]).astype(BF16), v_, TN), 0.0)
          for k_, v_, (ci, p) in zip(k, v, tiles)]
    entering = []
    state = [state_ref[p] for p in range(N_PAIRS)]
    for ci in range(nc):
        for p in range(N_PAIRS):
            entering.append(state[p])
            state[p] = state[p] * cd_ref[p] + kv[ci * N_PAIRS + p]
    for p in range(N_PAIRS):
        state_ref[p] = state[p]
    for n, (ci, p) in enumerate(tiles):
        cross = _dot((q[n] * xi_ref[:, p * LANES:(p + 1) * LANES]).astype(BF16), entering[n].astype(BF16))
        acc_ref[ci * c:(ci + 1) * c, p * LANES:(p + 1) * LANES] = inner[n] + cross
    o = acc_ref[...]
    o = o * lax.rsqrt(_seg_sum(o * o, bd_ref[...], split=False) * (1.0 / HEAD) + NORM_EPS)
    g = x_ref[0, :, 3 * WIDTH:4 * WIDTH].astype(F32)
    o_ref[0] = (o * (g * jax.nn.sigmoid(g))).astype(o_ref.dtype)


def _retention(proj, tables, bd):
    b, t, _ = proj.shape
    c = RET_CHUNK
    tb = RET_BLOCK
    cos, sin, dmat, xi, zeta, cd = tables
    return pl.pallas_call(
        _ret_kernel,
        grid=(b, t // tb),
        in_specs=[
            pl.BlockSpec((1, tb, 4 * WIDTH), lambda bi, n: (bi, n, COL_RET // (4 * WIDTH))),
            pl.BlockSpec((tb, WIDTH), lambda bi, n: (n, 0)),
            pl.BlockSpec((tb, WIDTH), lambda bi, n: (n, 0)),
            pl.BlockSpec((N_PAIRS, 2 * c, c), lambda bi, n: (0, 0, 0)),
            pl.BlockSpec((c, WIDTH), lambda bi, n: (0, 0)),
            pl.BlockSpec((c, WIDTH), lambda bi, n: (0, 0)),
            pl.BlockSpec((N_PAIRS, LANES, LANES), lambda bi, n: (0, 0, 0)),
            pl.BlockSpec((LANES, LANES), lambda bi, n: (0, 0)),
        ],
        out_specs=pl.BlockSpec((1, tb, WIDTH), lambda bi, n: (bi, n, 0)),
        out_shape=jax.ShapeDtypeStruct((b, t, WIDTH), BF16),
        scratch_shapes=[pltpu.VMEM((N_PAIRS, LANES, LANES), F32), pltpu.VMEM((tb, WIDTH), F32)],
        compiler_params=_params("arbitrary", "arbitrary"),
        name="retention",
    )(proj, cos, sin, dmat, xi, zeta, cd, bd)


def _retention_tables(t):
    c = RET_CHUNK
    half = HEAD // 2
    inv_freq = 1.0 / (ROPE_BASE ** (jnp.arange(half, dtype=F32) * 2.0 / HEAD))
    ang = jnp.arange(t, dtype=F32)[:, None] * inv_freq[None, :]
    cos = jnp.tile(jnp.concatenate([jnp.cos(ang), jnp.cos(ang)], axis=1), (1, N_HEADS))
    sin = jnp.tile(jnp.concatenate([-jnp.sin(ang), jnp.sin(ang)], axis=1), (1, N_HEADS))
    log_gamma = jnp.log1p(-jnp.power(2.0, -5.0 - jnp.arange(N_HEADS, dtype=F32)))
    idx = jnp.arange(c, dtype=F32)
    diff = idx[:, None] - idx[None, :]
    dmat = jnp.where(diff >= 0, jnp.exp(log_gamma[:, None, None] * jnp.maximum(diff, 0.0)), 0.0)
    xi = jnp.exp(log_gamma[:, None] * (idx[None, :] + 1.0))
    zeta = jnp.exp(log_gamma[:, None] * (c - 1.0 - idx[None, :]))
    chunk_decay = jnp.exp(log_gamma * c)
    xi = jnp.repeat(xi.T, HEAD, axis=1)
    zeta = jnp.repeat(zeta.T, HEAD, axis=1)
    cd = jnp.broadcast_to(jnp.repeat(chunk_decay, HEAD).reshape(N_PAIRS, LANES, 1), (N_PAIRS, LANES, LANES))
    dmat = dmat.reshape(N_PAIRS, 2 * c, c)
    return cos, sin, dmat, xi, zeta, cd


def _bdot(a, b, dims=NN):
    return _dot(a.astype(BF16), b.astype(BF16), dims)


def _rwkv_kernel(ops_ref, eend_ref, lng_ref, lnb_ref, bd_ref, o_ref, state_ref, y_ref):
    L = RWKV_CHUNK
    nrow, tb = ops_ref.shape[0], ops_ref.shape[1]
    nc = tb // L

    @pl.when(pl.program_id(1) == 0)
    def _():
        state_ref[...] = jnp.zeros_like(state_ref)

    bd = bd_ref[...]
    lane = lax.broadcasted_iota(jnp.int32, (L, LANES), 1)
    trow = lax.broadcasted_iota(jnp.int32, (L, LANES), 0)
    low = lane < HEAD
    tcol = lane & (HEAD - 1)
    strict = tcol < trow
    incl = tcol <= trow
    eye = (tcol == trow).astype(F32)
    rr = lax.broadcasted_iota(jnp.int32, (LANES, LANES), 0)
    cc = lax.broadcasted_iota(jnp.int32, (LANES, LANES), 1)
    same_head = (rr < HEAD) == (cc < HEAD)
    diag = rr == cc

    def bdiag(z):
        z = z.astype(BF16)
        zero = jnp.zeros_like(z)
        return jnp.concatenate([jnp.where(low, z, zero), jnp.where(low, zero, z)], axis=0)

    def bmm(x, z):
        return _dot(x.astype(BF16), bdiag(z))

    def stack(top, bottom):
        return jnp.concatenate([top, bottom], axis=0).astype(BF16)

    solved = {}

    def solve(units):
        tiles = [(b, c, p) for b, c in units for p in range(N_PAIRS)]

        def operand(slot):
            return [ops_ref[b, c * L:(c + 1) * L, slot * WIDTH + p * LANES:slot * WIDTH + (p + 1) * LANES]
                    for b, c, p in tiles]

        at_, rt_, bt_, kt_, vp_ = operand(OPS_A), operand(OPS_R), operand(OPS_B), operand(OPS_K), operand(OPS_V)
        bh, kh = operand(OPS_BH), operand(OPS_KH)
        tt = [_dot(stack(a_, r_), jnp.concatenate([bdiag(b_), bdiag(k_)], axis=0), NT)
              for a_, r_, b_, k_ in zip(at_, rt_, bt_, kt_)]
        yield
        t_ab = [jnp.where(strict, m[:L, :LANES], 0.0) for m in tt]
        t_rb = [jnp.where(incl, m[L:, :LANES], 0.0) for m in tt]
        t_ak = [jnp.where(strict, m[:L, LANES:], 0.0) for m in tt]
        t_rk = [jnp.where(incl, m[L:, LANES:], 0.0) for m in tt]
        inv = [eye + t for t in t_ab]
        pw = [bmm(t, t) for t in t_ab]
        yield
        for _ in range(4):
            st = [_dot(stack(w, i), bdiag(w)) for w, i in zip(pw, inv)]
            inv = [i + s[L:] for i, s in zip(inv, st)]
            pw = [s[:L] for s in st]
            yield
        inv = [i + bmm(i, w) for i, w in zip(inv, pw)]
        tvk = [_dot(stack(ta, tk), bdiag(v_)) for ta, tk, v_ in zip(t_ak, t_rk, vp_)]
        yield
        lhs = [stack(i, bmm(t, i)) for i, t in zip(inv, t_rb)]
        yield
        ra = [_dot(x, bdiag(a_)) for x, a_ in zip(lhs, at_)]
        ru = [_dot(x, bdiag(s[:L])) for x, s in zip(lhs, tvk)]
        yield
        qe = [r_.astype(F32) + s[L:] for r_, s in zip(rt_, ra)]
        yl = [s[L:] + s2[L:] for s, s2 in zip(ru, tvk)]
        e_end = [eend_ref[b, c, :, p * LANES:(p + 1) * LANES] for b, c, p in tiles]
        au = [_bdot(jnp.concatenate([s[:L], s2[:L]], axis=1), b_, TN) for s, s2, b_ in zip(ra, ru, bh)]
        pm = [jnp.where(same_head, s[:LANES], 0.0) + jnp.where(diag, e, 0.0) for s, e in zip(au, e_end)]
        gm = [jnp.where(same_head, s[LANES:] + _bdot(v_, k_, TN), 0.0) for s, v_, k_ in zip(au, vp_, kh)]
        for n, (b, c, p) in enumerate(tiles):
            solved.setdefault((b, c), {})[p] = (qe[n], yl[n], pm[n], gm[n])
        yield

    state = {(b, p): state_ref[b, p] for b in range(nrow) for p in range(N_PAIRS)}

    def scan_step(unit):
        b, c = unit
        for p in range(N_PAIRS):
            qe, yl, pm, gm = solved[unit][p]
            y_ref[b, c * L:(c + 1) * L, p * LANES:(p + 1) * LANES] = _bdot(qe, state[b, p], NT) + yl
            state[b, p] = _bdot(state[b, p], pm) + gm

    def finish(unit):
        b, c = unit
        rows = slice(c * L, (c + 1) * L)
        y = y_ref[b, rows]
        mean = _seg_sum(y, bd) * (1.0 / HEAD)
        yield
        d = y - mean
        var = _seg_sum(d * d, bd, split=False) * (1.0 / HEAD)
        yield
        yn = d * lax.rsqrt(var + RWKV_LN_EPS) * lng_ref[...] + lnb_ref[...]
        bonus = ops_ref[b, rows, OPS_BONUS * WIDTH:(OPS_BONUS + 1) * WIDTH].astype(F32)
        g = ops_ref[b, rows, OPS_G * WIDTH:(OPS_G + 1) * WIDTH].astype(F32)
        o_ref[b, rows] = ((yn + bonus) * g).astype(o_ref.dtype)

    active = []

    def tick(new=None):
        if new is not None:
            active.append(new)
        for work in list(active):
            if next(work, "done") == "done":
                active.remove(work)

    groups = [[(b, c) for c in range(c0, c0 + RWKV_GROUP)] for b in range(nrow) for c0 in range(0, nc, RWKV_GROUP)]
    scans, fins = [], []
    for gi, group in enumerate(groups):
        for stage, _ in enumerate(solve(group)):
            if stage % 2 == 1:
                if scans:
                    scan_step(scans.pop(0))
                tick(fins.pop(0) if fins else None)
        for unit in scans:
            scan_step(unit)
        scans = list(group)
        if gi > 0:
            fins += [finish(u) for u in groups[gi - 1]]
    for unit in scans:
        scan_step(unit)
        tick(fins.pop(0) if fins else None)
    for b in range(nrow):
        for p in range(N_PAIRS):
            state_ref[b, p] = state[b, p]
    while active or fins:
        tick(fins.pop(0) if fins else None)
    active.extend(finish(u) for u in groups[-1])
    while active:
        tick()


def _rwkv(ops, eend, lnx_g, lnx_b, bd):
    b, t, _ = ops.shape
    tb = RWKV_BLOCK
    nr = RWKV_ROWS
    nc = tb // RWKV_CHUNK
    row = pl.BlockSpec((1, WIDTH), lambda bi, n: (0, 0))
    return pl.pallas_call(
        _rwkv_kernel,
        grid=(b // nr, t // tb),
        in_specs=[
            pl.BlockSpec((nr, tb, OPS_WIDTH), lambda bi, n: (bi, n, 0)),
            pl.BlockSpec((nr, nc, 1, WIDTH), lambda bi, n: (bi, n, 0, 0)),
            row, row,
            pl.BlockSpec((LANES, LANES), lambda bi, n: (0, 0)),
        ],
        out_specs=pl.BlockSpec((nr, tb, WIDTH), lambda bi, n: (bi, n, 0)),
        out_shape=jax.ShapeDtypeStruct((b, t, WIDTH), BF16),
        scratch_shapes=[
            pltpu.VMEM((nr, N_PAIRS, LANES, LANES), F32),
            pltpu.VMEM((nr, tb, WIDTH), F32),
        ],
        compiler_params=_params("arbitrary", "arbitrary"),
        name="rwkv7",
    )(ops, eend, lnx_g, lnx_b, bd)


def _merge_kernel(oa_ref, ob_ref, oc_ref, g0_ref, g1_ref, g2_ref, x_ref, wa_ref, wb_ref, wc_ref, wo_ref, o_ref):
    def branch(o, w, g):
        return jax.nn.sigmoid(g[...].astype(F32)) * _dot(o[...], w[...])

    mixed = branch(oa_ref, wa_ref, g0_ref) + branch(ob_ref, wb_ref, g1_ref) + branch(oc_ref, wc_ref, g2_ref)
    o_ref[...] = x_ref[...] + _dot(mixed.astype(BF16), wo_ref[...])


def _merge(x, proj, oa, ob, oc, wa, wb, wc, wo, layer, tm):
    n, d = x.shape
    branch = pl.BlockSpec((tm, WIDTH), lambda i: (i, 0))
    gate = lambda j: pl.BlockSpec((tm, d), lambda i: (i, COL_GATES // d + j))
    return pl.pallas_call(
        _merge_kernel,
        grid=(n // tm,),
        in_specs=[branch, branch, branch, gate(0), gate(1), gate(2),
                  pl.BlockSpec((tm, d), lambda i: (i, 0)),
                  _resident((WIDTH, d), layer), _resident((WIDTH, d), layer), _resident((WIDTH, d), layer),
                  _resident((d, d), layer)],
        out_specs=pl.BlockSpec((tm, d), lambda i: (i, 0)),
        out_shape=jax.ShapeDtypeStruct((n, d), F32),
        compiler_params=_params("arbitrary"),
        name="merge_out_proj",
    )(oa, ob, oc, proj, proj, proj, x, wa, wb, wc, wo)


def _ffn_kernel(x_ref, g_ref, wg_ref, wu_ref, wd_ref, o_ref):
    x = x_ref[...]
    h = (x * g_ref[...]).astype(BF16)
    rs = lax.rsqrt(jnp.mean(x * x, axis=-1, keepdims=True) + NORM_EPS)
    gate = _dot(h, wg_ref[...]) * rs
    up = _dot(h, wu_ref[...]) * rs
    act = (gate * jax.nn.sigmoid(gate) * up).astype(BF16)
    o_ref[...] = x + _dot(act, wd_ref[...])


def _ffn(x, g, wg, wu, wd, layer, tm):
    n, d = x.shape
    f = wg.shape[2]
    return pl.pallas_call(
        _ffn_kernel,
        grid=(n // tm,),
        in_specs=[pl.BlockSpec((tm, d), lambda i: (i, 0)), _resident((1, d)),
                  _resident((d, f), layer), _resident((d, f), layer), _resident((f, d), layer)],
        out_specs=pl.BlockSpec((tm, d), lambda i: (i, 0)),
        out_shape=jax.ShapeDtypeStruct((n, d), F32),
        compiler_params=_params("arbitrary"),
        name="swiglu_ffn",
    )(x, g, wg, wu, wd)


def _lora_weight(w2, a2, g2):
    w = jnp.zeros((2 * LANES, 3 * WIDTH), F32)
    w = w.at[0:64, 0:WIDTH].set(w2)
    w = w.at[64:128, WIDTH:2 * WIDTH].set(a2)
    w = w.at[128:256, 2 * WIDTH:3 * WIDTH].set(g2)
    return w.astype(BF16)


def _layer(x, consts, layer, big, norm1_g, attn_q_norm_g, attn_k_norm_g, attn_sinks,
           rwkv_shift_mu, rwkv_w0, rwkv_w2, rwkv_a0, rwkv_a2, rwkv_g2, rwkv_k_k, rwkv_k_a,
           rwkv_r_k, rwkv_lnx_g, rwkv_lnx_b, norm2_g):
    b, t, d = x.shape
    n = b * t
    bd, tri, ret_tables = consts
    w_in, w_attn_o, w_rwkv_o, w_ret_o, w_out, w_ffn_gate, w_ffn_up, w_ffn_down = big
    x2 = x.reshape(n, d)
    row = lambda a: a.reshape(1, WIDTH)
    prm = (rwkv_shift_mu.reshape(1, -1), row(rwkv_w0), row(rwkv_a0), row(rwkv_k_k), row(rwkv_k_a), row(rwkv_r_k),
           _lora_weight(rwkv_w2, rwkv_a2, rwkv_g2))
    tm = IN_PROJ_ROWS
    proj, ops, eend = _in_proj(x2, norm1_g.reshape(1, d), w_in, layer, prm, tri, bd, tm, t // tm)
    proj3 = proj.reshape(b, t, MAIN_WIDTH)

    o_a = _attention(proj3, attn_sinks, jnp.tile(attn_q_norm_g, N_HEADS).reshape(1, WIDTH),
                     jnp.tile(attn_k_norm_g, 2).reshape(1, LANES), bd, tq=ATTN_ROWS)

    o_b = _rwkv(ops.reshape(b, t, OPS_WIDTH), eend.reshape(b, t // RWKV_CHUNK, 1, WIDTH),
                row(rwkv_lnx_g), row(rwkv_lnx_b), bd)

    o_c = _retention(proj3, ret_tables, bd)

    x2 = _merge(x2, proj, o_a.reshape(n, WIDTH), o_b.reshape(n, WIDTH), o_c.reshape(n, WIDTH),
                w_attn_o, w_rwkv_o, w_ret_o, w_out, layer, tm=MERGE_ROWS)
    x2 = _ffn(x2, norm2_g.reshape(1, d), w_ffn_gate, w_ffn_up, w_ffn_down, layer, tm=FFN_ROWS)
    return x2.reshape(b, t, d)


def _constants(t):
    head_id = jnp.arange(LANES) // HEAD
    bd = (head_id[:, None] == head_id[None, :]).astype(BF16)
    idx = jnp.arange(RWKV_CHUNK)
    tri = (idx[None, :] <= idx[:, None]).astype(BF16)
    return bd, tri, _retention_tables(t)


def kernel(x, norm1_g, w_in, attn_q_norm_g, attn_k_norm_g, attn_sinks, w_attn_o, rwkv_shift_mu, rwkv_w0, rwkv_w2, rwkv_a0, rwkv_a2, rwkv_g2, rwkv_k_k, rwkv_k_a, rwkv_r_k, rwkv_lnx_g, rwkv_lnx_b, w_rwkv_o, w_ret_o, w_out, norm2_g, w_ffn_gate, w_ffn_up, w_ffn_down):
    consts = _constants(x.shape[1])
    big = tuple(w.astype(BF16) for w in (w_in, w_attn_o, w_rwkv_o, w_ret_o, w_out, w_ffn_gate, w_ffn_up, w_ffn_down))
    small = (norm1_g, attn_q_norm_g, attn_k_norm_g, attn_sinks, rwkv_shift_mu, rwkv_w0, rwkv_w2, rwkv_a0, rwkv_a2,
             rwkv_g2, rwkv_k_k, rwkv_k_a, rwkv_r_k, rwkv_lnx_g, rwkv_lnx_b, norm2_g)
    for layer in range(norm1_g.shape[0]):
        x = _layer(x, consts, layer, big, *(w[layer] for w in small))
    return x
```

```python
import functools

import jax
import jax.numpy as jnp
from jax import lax
from jax.experimental import pallas as pl
from jax.experimental.pallas import tpu as pltpu

F32 = jnp.float32
BF16 = jnp.bfloat16

HEAD = 64
LANES = 128
D_MODEL = 1024
N_HEADS = 8
WIDTH = N_HEADS * HEAD
N_PAIRS = WIDTH // LANES
ATTN_BLOCK = 128
ATTN_ROWS = 1024
MERGE_ROWS = 1024
MERGE_CHUNK = 512
FFN_ROWS = 512
RET_CHUNK = 128
RET_BLOCK = 1024
RWKV_CHUNK = 64
RWKV_BLOCK = 512
RWKV_ROWS = 2
RWKV_GROUP = 4
NORM_EPS = 1e-6
RWKV_LN_EPS = 64e-5
ROPE_BASE = 10000.0
VMEM_LIMIT = 56 * 1024 * 1024

COL_RET = 0
COL_GATES = 2048
COL_ATTN_Q = 5120
COL_ATTN_K = 5632
COL_ATTN_V = 5760
MAIN_WIDTH = 5888
SRC_ATTN = 0
SRC_RWKV = 768
SRC_RET = 2560
RWKV_COLS = 1792
OPS_A, OPS_R, OPS_B, OPS_K, OPS_BH, OPS_KH, OPS_V, OPS_G, OPS_BONUS = range(9)
OPS_WIDTH = 9 * WIDTH

NN = (((1,), (0,)), ((), ()))
NT = (((1,), (1,)), ((), ()))
TN = (((0,), (0,)), ((), ()))


def _dot(a, b, dims=NN, precision=None):
    return lax.dot_general(a, b, dims, precision=precision, preferred_element_type=F32)


def _seg_sum(x, bd, split=True):
    cols = []
    for j in range(x.shape[1] // LANES):
        xc = x[:, j * LANES:(j + 1) * LANES]
        hi = xc.astype(BF16)
        s = _dot(hi, bd)
        if split:
            s = s + _dot((xc - hi.astype(F32)).astype(BF16), bd)
        cols.append(s)
    return cols[0] if len(cols) == 1 else jnp.concatenate(cols, axis=1)


def _params(*sem):
    return pltpu.CompilerParams(dimension_semantics=sem, vmem_limit_bytes=VMEM_LIMIT)


def _resident(shape, layer=None):
    nd = len(shape)
    if layer is None:
        return pl.BlockSpec(shape, lambda *_: (0,) * nd, pipeline_mode=pl.Buffered(1))
    return pl.BlockSpec((None,) + tuple(shape), lambda *_: (layer,) + (0,) * nd, pipeline_mode=pl.Buffered(1))


IN_PROJ_ROWS = 512
PREP_ROWS = 256
PIECE = 256
LEAD_PIECES = 5


def _main_pieces():
    pieces = [(SRC_RET + d, d) for d in range(0, COL_ATTN_Q, PIECE)]
    pieces += [(SRC_ATTN + d, COL_ATTN_Q + d) for d in range(0, MAIN_WIDTH - COL_ATTN_Q, PIECE)]
    return pieces


def _in_proj_kernel(x_ref, g_ref, w_ref, mu_ref, w0_ref, a0_ref, kk_ref, ka_ref, rk_ref, wl_ref, tri_ref, bd_ref,
                    main_ref, ops_ref, eend_ref, carry_ref, *, blocks_per_row):
    tm = x_ref.shape[0]
    L = RWKV_CHUNK

    @pl.when(pl.program_id(0) == 0)
    def _():
        carry_ref[...] = jnp.zeros_like(carry_ref)

    x = x_ref[...]
    h = (x * g_ref[...]).astype(BF16)
    rs = lax.rsqrt(jnp.mean(x * x, axis=-1, keepdims=True) + NORM_EPS)
    hr = _dot(h, w_ref[:, SRC_RWKV:SRC_RWKV + RWKV_COLS]) * rs
    bd = bd_ref[...]
    tri = tri_ref[...]
    prev_row = jnp.where(pl.program_id(0) % blocks_per_row == 0, 0.0, carry_ref[0:1, :])
    carry_ref[0:1, :] = hr[tm - 1:tm]

    def put(rows, slot, val):
        ops_ref[rows, slot * WIDTH:(slot + 1) * WIDTH] = val.astype(BF16)

    pieces = iter(_main_pieces())

    def emit_pieces(count):
        for _ in range(count):
            piece = next(pieces, None)
            if piece is not None:
                src, dst = piece
                main_ref[:, dst:dst + PIECE] = (_dot(h, w_ref[:, src:src + PIECE]) * rs).astype(main_ref.dtype)

    groups = tm // PREP_ROWS
    emit_pieces(LEAD_PIECES)
    per_stage = -(-(len(_main_pieces()) - LEAD_PIECES) // (2 * groups))
    for gi in range(groups):
        r0 = gi * PREP_ROWS
        rows = slice(r0, r0 + PREP_ROWS)
        hb = hr[rows]
        row = lax.broadcasted_iota(jnp.int32, hb.shape, 0)
        prev = jnp.where(row == 0, prev_row, pltpu.roll(hb, 1, 0))
        prev_row = hb[PREP_ROWS - 1:PREP_ROWS]
        z = hb + mu_ref[...] * (prev - hb)
        r, k, v, lo = z[:, 0:WIDTH], z[:, WIDTH:2 * WIDTH], z[:, 2 * WIDTH:3 * WIDTH], z[:, 3 * WIDTH:]
        lane_lo = lax.broadcasted_iota(jnp.int32, lo.shape, 1)
        act = jnp.where(lane_lo < HEAD, jnp.tanh(lo), jnp.where(lane_lo < 2 * HEAD, lo, jax.nn.sigmoid(lo)))
        kkr = k * kk_ref[...]
        lora = _dot(act.astype(BF16), wl_ref[...])
        kk_norm = _seg_sum(kkr * kkr, bd, split=False)
        emit_pieces(per_stage)
        wx = -(w0_ref[...] + lora[:, 0:WIDTH])
        softplus = jnp.maximum(wx, 0.0) + jnp.log(1.0 + jnp.exp(-jnp.abs(wx)))
        lw = -jnp.exp(-softplus - 0.5)
        a = jax.nn.sigmoid(a0_ref[...] + lora[:, WIDTH:2 * WIDTH])
        kk = kkr / jnp.maximum(jnp.sqrt(kk_norm), 1e-12)
        k2 = k * (1.0 + (a - 1.0) * ka_ref[...])
        lw_hi = lw.astype(BF16)
        lw_lo = (lw - lw_hi.astype(F32)).astype(BF16)
        chunks = [slice(c0, c0 + L) for c0 in range(0, PREP_ROWS, L)]
        cums = [_dot(tri, lw_hi[cs]) + _dot(tri, lw_lo[cs]) for cs in chunks]
        bonus = _seg_sum(r * k2 * rk_ref[...], bd) * v
        emit_pieces(per_stage)
        put(rows, OPS_V, v)
        put(rows, OPS_G, lora[:, 2 * WIDTH:3 * WIDTH])
        put(rows, OPS_BONUS, bonus)
        for cs, cum in zip(chunks, cums):
            crow = slice(r0 + cs.start, r0 + cs.stop)
            e_pos = jnp.exp(cum)
            e_neg = jnp.exp(-cum)
            e_end = e_pos[L - 1:L]
            b_t = kk[cs] * a[cs] * e_neg
            k_t = k2[cs] * e_neg
            put(crow, OPS_A, -kk[cs] * jnp.exp(cum - lw[cs]))
            put(crow, OPS_R, r[cs] * e_pos)
            put(crow, OPS_B, b_t)
            put(crow, OPS_K, k_t)
            put(crow, OPS_BH, b_t * e_end)
            put(crow, OPS_KH, k_t * e_end)
            eend_ref[crow.start // L:crow.start // L + 1, :] = e_end
    emit_pieces(len(_main_pieces()))


def _in_proj(x, g, w, layer, prm, tri, bd, tm, blocks_per_row):
    n, d = x.shape
    row = lambda width: _resident((1, width))
    return pl.pallas_call(
        functools.partial(_in_proj_kernel, blocks_per_row=blocks_per_row),
        grid=(n // tm,),
        in_specs=[pl.BlockSpec((tm, d), lambda i: (i, 0)), row(d), _resident(w.shape[1:], layer),
                  row(RWKV_COLS), row(WIDTH), row(WIDTH), row(WIDTH), row(WIDTH), row(WIDTH),
                  _resident((2 * LANES, 3 * WIDTH)), _resident((RWKV_CHUNK, RWKV_CHUNK)), _resident((LANES, LANES))],
        out_specs=[pl.BlockSpec((tm, MAIN_WIDTH), lambda i: (i, 0)),
                   pl.BlockSpec((tm, OPS_WIDTH), lambda i: (i, 0)),
                   pl.BlockSpec((tm // RWKV_CHUNK, WIDTH), lambda i: (i, 0))],
        out_shape=[jax.ShapeDtypeStruct((n, MAIN_WIDTH), BF16),
                   jax.ShapeDtypeStruct((n, OPS_WIDTH), BF16),
                   jax.ShapeDtypeStruct((n // RWKV_CHUNK, WIDTH), F32)],
        scratch_shapes=[pltpu.VMEM((8, RWKV_COLS), F32)],
        compiler_params=_params("arbitrary"),
        name="in_proj",
    )(x, g, w, *prm, tri, bd)


def _attn_kernel(sink_ref, q_ref, kc_ref, vc_ref, kp_ref, vp_ref, gq_ref, gk_ref, bd_ref, o_ref, *, tq):
    i = pl.program_id(1)
    bd = bd_ref[...]
    q = q_ref[0].astype(F32)
    qn = q * lax.rsqrt(_seg_sum(q * q, bd, split=False) * (1.0 / HEAD) + NORM_EPS) * (gq_ref[...] * HEAD ** -0.5)
    k = jnp.concatenate([kp_ref[0], kc_ref[0]], axis=0).astype(F32)
    kn = k * lax.rsqrt(_seg_sum(k * k, bd, split=False) * (1.0 / HEAD) + NORM_EPS) * gk_ref[...]
    v = jnp.concatenate([vp_ref[0], vc_ref[0]], axis=0).astype(F32)

    lane_k = lax.broadcasted_iota(jnp.int32, k.shape, 1)
    lane_q = lax.broadcasted_iota(jnp.int32, (ATTN_BLOCK, LANES), 1)
    low_q = lane_q < HEAD
    key_row = lax.broadcasted_iota(jnp.int32, (ATTN_BLOCK, 2 * ATTN_BLOCK), 0)
    query_col = lax.broadcasted_iota(jnp.int32, (ATTN_BLOCK, 2 * ATTN_BLOCK), 1)
    from_prev = key_row > (query_col & (ATTN_BLOCK - 1))
    first_head = lax.broadcasted_iota(jnp.int32, (1, 2 * ATTN_BLOCK), 1) < ATTN_BLOCK

    k_dup, v_lo, v_hi = [], [], []
    for e in range(2):
        in_e = (lane_k >= e * HEAD) & (lane_k < (e + 1) * HEAD)
        k_e = jnp.where(in_e, kn, 0.0)
        k_dup.append((k_e + pltpu.roll(k_e, HEAD, 1)).astype(BF16))
        v_e = jnp.where(in_e, v, 0.0)
        lo = v_e if e == 0 else pltpu.roll(v_e, HEAD, 1)
        v_lo.append(lo.astype(BF16))
        v_hi.append(pltpu.roll(lo, HEAD, 1).astype(BF16))

    nb = tq // ATTN_BLOCK
    tiles = [(n, c) for n in range(nb) for c in range(N_PAIRS)]

    def keys(arr, n, c):
        return arr[c // 2][n * ATTN_BLOCK:(n + 2) * ATTN_BLOCK]

    qs = []
    for n, c in tiles:
        qc = qn[n * ATTN_BLOCK:(n + 1) * ATTN_BLOCK, c * LANES:(c + 1) * LANES]
        qs.append(jnp.concatenate([jnp.where(low_q, qc, 0.0), jnp.where(low_q, 0.0, qc)], axis=0).astype(BF16))
    s = [_dot(keys(k_dup, n, c), q_, NT) for q_, (n, c) in zip(qs, tiles)]
    no_prev = jnp.where(i == 0, -jnp.inf, 0.0)
    s = [jnp.where(from_prev, s_[:ATTN_BLOCK] + no_prev if n == 0 else s_[:ATTN_BLOCK], s_[ATTN_BLOCK:])
         for s_, (n, c) in zip(s, tiles)]
    sink = [jnp.where(first_head, sink_ref[2 * c], sink_ref[2 * c + 1]) for n, c in tiles]
    m = [jnp.maximum(jnp.max(s_, axis=0, keepdims=True), k_) for s_, k_ in zip(s, sink)]
    p = [jnp.exp(s_ - m_) for s_, m_ in zip(s, m)]
    den = [jnp.sum(p_, axis=0, keepdims=True) + jnp.exp(k_ - m_) for p_, k_, m_ in zip(p, sink, m)]
    p = [(p_ * (1.0 / d_)).astype(BF16) for p_, d_ in zip(p, den)]
    zero = jnp.zeros_like(p[0])
    pb = [jnp.concatenate([jnp.where(from_prev, p_, zero), jnp.where(from_prev, zero, p_)], axis=0) for p_ in p]
    for p_, (n, c) in zip(pb, tiles):
        o = _dot(p_[:, :ATTN_BLOCK], keys(v_lo, n, c), TN) + _dot(p_[:, ATTN_BLOCK:], keys(v_hi, n, c), TN)
        o_ref[0, n * ATTN_BLOCK:(n + 1) * ATTN_BLOCK, c * LANES:(c + 1) * LANES] = o.astype(o_ref.dtype)


def _attention(proj, sinks, gq, gk, bd, tq):
    b, t, _ = proj.shape
    nb = tq // ATTN_BLOCK
    kernel = functools.partial(_attn_kernel, tq=tq)
    return pl.pallas_call(
        kernel,
        grid=(b, t // tq),
        in_specs=[
            pl.BlockSpec(memory_space=pltpu.SMEM),
            pl.BlockSpec((1, tq, WIDTH), lambda bi, i: (bi, i, COL_ATTN_Q // WIDTH)),
            pl.BlockSpec((1, tq, LANES), lambda bi, i: (bi, i, COL_ATTN_K // LANES)),
            pl.BlockSpec((1, tq, LANES), lambda bi, i: (bi, i, COL_ATTN_V // LANES)),
            pl.BlockSpec((1, ATTN_BLOCK, LANES),
                         lambda bi, i: (bi, jnp.maximum(i * nb - 1, 0), COL_ATTN_K // LANES)),
            pl.BlockSpec((1, ATTN_BLOCK, LANES),
                         lambda bi, i: (bi, jnp.maximum(i * nb - 1, 0), COL_ATTN_V // LANES)),
            pl.BlockSpec((1, WIDTH), lambda bi, i: (0, 0)),
            pl.BlockSpec((1, LANES), lambda bi, i: (0, 0)),
            pl.BlockSpec((LANES, LANES), lambda bi, i: (0, 0)),
        ],
        out_specs=pl.BlockSpec((1, tq, WIDTH), lambda bi, i: (bi, i, 0)),
        out_shape=jax.ShapeDtypeStruct((b, t, WIDTH), BF16),
        compiler_params=_params("arbitrary", "arbitrary"),
        name="swa_attention",
    )(sinks, proj, proj, proj, proj, proj, gq, gk, bd)


def _ret_kernel(x_ref, cos_ref, sin_ref, dmat_ref, xi_ref, zeta_ref, cd_ref, bd_ref, o_ref, state_ref, acc_ref):
    @pl.when(pl.program_id(1) == 0)
    def _():
        state_ref[...] = jnp.zeros_like(state_ref)

    c = RET_CHUNK
    nc = x_ref.shape[1] // c
    lane = lax.broadcasted_iota(jnp.int32, (c, LANES), 1)
    first_half = (lane & (HEAD - 1)) < HEAD // 2
    low = lane < HEAD
    rr = lax.broadcasted_iota(jnp.int32, (LANES, LANES), 0)
    cc = lax.broadcasted_iota(jnp.int32, (LANES, LANES), 1)
    same_head = (rr < HEAD) == (cc < HEAD)
    tiles = [(ci, p) for ci in range(nc) for p in range(N_PAIRS)]

    def rotary(col, ci, p):
        rows = slice(ci * c, (ci + 1) * c)
        x = x_ref[0, rows, col + p * LANES:col + (p + 1) * LANES].astype(F32)
        swapped = jnp.where(first_half, pltpu.roll(x, LANES - HEAD // 2, 1), pltpu.roll(x, HEAD // 2, 1))
        sl = slice(p * LANES, (p + 1) * LANES)
        return x * cos_ref[rows, sl] + swapped * sin_ref[rows, sl]

    def split_heads(x):
        zero = jnp.zeros_like(x)
        return jnp.concatenate([jnp.where(low, x, zero), jnp.where(low, zero, x)], axis=0)

    q = [rotary(0, ci, p) for ci, p in tiles]
    k = [rotary(WIDTH, ci, p) * HEAD ** -0.5 for ci, p in tiles]
    v = [x_ref[0, ci * c:(ci + 1) * c, 2 * WIDTH + p * LANES:2 * WIDTH + (p + 1) * LANES] for ci, p in tiles]
    s = [_dot(split_heads(q_.astype(BF16)), k_.astype(BF16), NT) * dmat_ref[p]
         for q_, k_, (ci, p) in zip(q, k, tiles)]
    inner = [_dot(jnp.concatenate([s_[:c], s_[c:]], axis=1).astype(BF16), split_heads(v_)) for s_, v_ in zip(s, v)]
    kv = [jnp.where(same_head, _dot((k_ * zeta_ref[:, p * LANES:(p + 1) * LANES]).astype(BF16), v_, TN), 0.0)
          for k_, v_, (ci, p) in zip(k, v, tiles)]
    entering = []
    state = [state_ref[p] for p in range(N_PAIRS)]
    for ci in range(nc):
        for p in range(N_PAIRS):
            entering.append(state[p])
            state[p] = state[p] * cd_ref[p] + kv[ci * N_PAIRS + p]
    for p in range(N_PAIRS):
        state_ref[p] = state[p]
    for n, (ci, p) in enumerate(tiles):
        cross = _dot((q[n] * xi_ref[:, p * LANES:(p + 1) * LANES]).astype(BF16), entering[n].astype(BF16))
        acc_ref[ci * c:(ci + 1) * c, p * LANES:(p + 1) * LANES] = inner[n] + cross
    o = acc_ref[...]
    o = o * lax.rsqrt(_seg_sum(o * o, bd_ref[...], split=False) * (1.0 / HEAD) + NORM_EPS)
    g = x_ref[0, :, 3 * WIDTH:4 * WIDTH].astype(F32)
    o_ref[0] = (o * (g * jax.nn.sigmoid(g))).astype(o_ref.dtype)


def _retention(proj, tables, bd):
    b, t, _ = proj.shape
    c = RET_CHUNK
    tb = RET_BLOCK
    cos, sin, dmat, xi, zeta, cd = tables
    return pl.pallas_call(
        _ret_kernel,
        grid=(b, t // tb),
        in_specs=[
            pl.BlockSpec((1, tb, 4 * WIDTH), lambda bi, n: (bi, n, COL_RET // (4 * WIDTH))),
            pl.BlockSpec((tb, WIDTH), lambda bi, n: (n, 0)),
            pl.BlockSpec((tb, WIDTH), lambda bi, n: (n, 0)),
            pl.BlockSpec((N_PAIRS, 2 * c, c), lambda bi, n: (0, 0, 0)),
            pl.BlockSpec((c, WIDTH), lambda bi, n: (0, 0)),
            pl.BlockSpec((c, WIDTH), lambda bi, n: (0, 0)),
            pl.BlockSpec((N_PAIRS, LANES, LANES), lambda bi, n: (0, 0, 0)),
            pl.BlockSpec((LANES, LANES), lambda bi, n: (0, 0)),
        ],
        out_specs=pl.BlockSpec((1, tb, WIDTH), lambda bi, n: (bi, n, 0)),
        out_shape=jax.ShapeDtypeStruct((b, t, WIDTH), BF16),
        scratch_shapes=[pltpu.VMEM((N_PAIRS, LANES, LANES), F32), pltpu.VMEM((tb, WIDTH), F32)],
        compiler_params=_params("arbitrary", "arbitrary"),
        name="retention",
    )(proj, cos, sin, dmat, xi, zeta, cd, bd)


def _retention_tables(t):
    c = RET_CHUNK
    half = HEAD // 2
    inv_freq = 1.0 / (ROPE_BASE ** (jnp.arange(half, dtype=F32) * 2.0 / HEAD))
    ang = jnp.arange(t, dtype=F32)[:, None] * inv_freq[None, :]
    cos = jnp.tile(jnp.concatenate([jnp.cos(ang), jnp.cos(ang)], axis=1), (1, N_HEADS))
    sin = jnp.tile(jnp.concatenate([-jnp.sin(ang), jnp.sin(ang)], axis=1), (1, N_HEADS))
    log_gamma = jnp.log1p(-jnp.power(2.0, -5.0 - jnp.arange(N_HEADS, dtype=F32)))
    idx = jnp.arange(c, dtype=F32)
    diff = idx[:, None] - idx[None, :]
    dmat = jnp.where(diff >= 0, jnp.exp(log_gamma[:, None, None] * jnp.maximum(diff, 0.0)), 0.0)
    xi = jnp.exp(log_gamma[:, None] * (idx[None, :] + 1.0))
    zeta = jnp.exp(log_gamma[:, None] * (c - 1.0 - idx[None, :]))
    chunk_decay = jnp.exp(log_gamma * c)
    xi = jnp.repeat(xi.T, HEAD, axis=1)
    zeta = jnp.repeat(zeta.T, HEAD, axis=1)
    cd = jnp.broadcast_to(jnp.repeat(chunk_decay, HEAD).reshape(N_PAIRS, LANES, 1), (N_PAIRS, LANES, LANES))
    dmat = dmat.reshape(N_PAIRS, 2 * c, c)
    return cos, sin, dmat, xi, zeta, cd


def _bdot(a, b, dims=NN):
    return _dot(a.astype(BF16), b.astype(BF16), dims)


def _rwkv_kernel(ops_ref, eend_ref, lng_ref, lnb_ref, bd_ref, o_ref, state_ref, y_ref):
    L = RWKV_CHUNK
    nrow, tb = ops_ref.shape[0], ops_ref.shape[1]
    nc = tb // L

    @pl.when(pl.program_id(1) == 0)
    def _():
        state_ref[...] = jnp.zeros_like(state_ref)

    bd = bd_ref[...]
    lane = lax.broadcasted_iota(jnp.int32, (L, LANES), 1)
    trow = lax.broadcasted_iota(jnp.int32, (L, LANES), 0)
    low = lane < HEAD
    tcol = lane & (HEAD - 1)
    strict = tcol < trow
    incl = tcol <= trow
    eye = (tcol == trow).astype(F32)
    rr = lax.broadcasted_iota(jnp.int32, (LANES, LANES), 0)
    cc = lax.broadcasted_iota(jnp.int32, (LANES, LANES), 1)
    same_head = (rr < HEAD) == (cc < HEAD)
    diag = rr == cc

    def bdiag(z):
        z = z.astype(BF16)
        zero = jnp.zeros_like(z)
        return jnp.concatenate([jnp.where(low, z, zero), jnp.where(low, zero, z)], axis=0)

    def bmm(x, z):
        return _dot(x.astype(BF16), bdiag(z))

    def stack(top, bottom):
        return jnp.concatenate([top, bottom], axis=0).astype(BF16)

    solved = {}

    def solve(units):
        tiles = [(b, c, p) for b, c in units for p in range(N_PAIRS)]

        def operand(slot):
            return [ops_ref[b, c * L:(c + 1) * L, slot * WIDTH + p * LANES:slot * WIDTH + (p + 1) * LANES]
                    for b, c, p in tiles]

        at_, rt_, bt_, kt_, vp_ = operand(OPS_A), operand(OPS_R), operand(OPS_B), operand(OPS_K), operand(OPS_V)
        bh, kh = operand(OPS_BH), operand(OPS_KH)
        tt = [_dot(stack(a_, r_), jnp.concatenate([bdiag(b_), bdiag(k_)], axis=0), NT)
              for a_, r_, b_, k_ in zip(at_, rt_, bt_, kt_)]
        yield
        t_ab = [jnp.where(strict, m[:L, :LANES], 0.0) for m in tt]
        t_rb = [jnp.where(incl, m[L:, :LANES], 0.0) for m in tt]
        t_ak = [jnp.where(strict, m[:L, LANES:], 0.0) for m in tt]
        t_rk = [jnp.where(incl, m[L:, LANES:], 0.0) for m in tt]
        inv = [eye + t for t in t_ab]
        pw = [bmm(t, t) for t in t_ab]
        yield
        for _ in range(4):
            st = [_dot(stack(w, i), bdiag(w)) for w, i in zip(pw, inv)]
            inv = [i + s[L:] for i, s in zip(inv, st)]
            pw = [s[:L] for s in st]
            yield
        inv = [i + bmm(i, w) for i, w in zip(inv, pw)]
        tvk = [_dot(stack(ta, tk), bdiag(v_)) for ta, tk, v_ in zip(t_ak, t_rk, vp_)]
        yield
        lhs = [stack(i, bmm(t, i)) for i, t in zip(inv, t_rb)]
        yield
        ra = [_dot(x, bdiag(a_)) for x, a_ in zip(lhs, at_)]
        ru = [_dot(x, bdiag(s[:L])) for x, s in zip(lhs, tvk)]
        yield
        qe = [r_.astype(F32) + s[L:] for r_, s in zip(rt_, ra)]
        yl = [s[L:] + s2[L:] for s, s2 in zip(ru, tvk)]
        e_end = [eend_ref[b, c, :, p * LANES:(p + 1) * LANES] for b, c, p in tiles]
        au = [_bdot(jnp.concatenate([s[:L], s2[:L]], axis=1), b_, TN) for s, s2, b_ in zip(ra, ru, bh)]
        pm = [jnp.where(same_head, s[:LANES], 0.0) + jnp.where(diag, e, 0.0) for s, e in zip(au, e_end)]
        gm = [jnp.where(same_head, s[LANES:] + _bdot(v_, k_, TN), 0.0) for s, v_, k_ in zip(au, vp_, kh)]
        for n, (b, c, p) in enumerate(tiles):
            solved.setdefault((b, c), {})[p] = (qe[n], yl[n], pm[n], gm[n])
        yield

    state = {(b, p): state_ref[b, p] for b in range(nrow) for p in range(N_PAIRS)}

    def scan_step(unit):
        b, c = unit
        for p in range(N_PAIRS):
            qe, yl, pm, gm = solved[unit][p]
            y_ref[b, c * L:(c + 1) * L, p * LANES:(p + 1) * LANES] = _bdot(qe, state[b, p], NT) + yl
            state[b, p] = _bdot(state[b, p], pm) + gm

    def finish(unit):
        b, c = unit
        rows = slice(c * L, (c + 1) * L)
        y = y_ref[b, rows]
        mean = _seg_sum(y, bd) * (1.0 / HEAD)
        yield
        d = y - mean
        var = _seg_sum(d * d, bd, split=False) * (1.0 / HEAD)
        yield
        yn = d * lax.rsqrt(var + RWKV_LN_EPS) * lng_ref[...] + lnb_ref[...]
        bonus = ops_ref[b, rows, OPS_BONUS * WIDTH:(OPS_BONUS + 1) * WIDTH].astype(F32)
        g = ops_ref[b, rows, OPS_G * WIDTH:(OPS_G + 1) * WIDTH].astype(F32)
        o_ref[b, rows] = ((yn + bonus) * g).astype(o_ref.dtype)

    active = []

    def tick(new=None):
        if new is not None:
            active.append(new)
        for work in list(active):
            if next(work, "done") == "done":
                active.remove(work)

    groups = [[(b, c) for c in range(c0, c0 + RWKV_GROUP)] for b in range(nrow) for c0 in range(0, nc, RWKV_GROUP)]
    scans, fins = [], []
    for gi, group in enumerate(groups):
        for stage, _ in enumerate(solve(group)):
            if stage % 2 == 1:
                if scans:
                    scan_step(scans.pop(0))
                tick(fins.pop(0) if fins else None)
        for unit in scans:
            scan_step(unit)
        scans = list(group)
        if gi > 0:
            fins += [finish(u) for u in groups[gi - 1]]
    for unit in scans:
        scan_step(unit)
        tick(fins.pop(0) if fins else None)
    for b in range(nrow):
        for p in range(N_PAIRS):
            state_ref[b, p] = state[b, p]
    while active or fins:
        tick(fins.pop(0) if fins else None)
    active.extend(finish(u) for u in groups[-1])
    while active:
        tick()


def _rwkv(ops, eend, lnx_g, lnx_b, bd):
    b, t, _ = ops.shape
    tb = RWKV_BLOCK
    nr = RWKV_ROWS
    nc = tb // RWKV_CHUNK
    row = pl.BlockSpec((1, WIDTH), lambda bi, n: (0, 0))
    return pl.pallas_call(
        _rwkv_kernel,
        grid=(b // nr, t // tb),
        in_specs=[
            pl.BlockSpec((nr, tb, OPS_WIDTH), lambda bi, n: (bi, n, 0)),
            pl.BlockSpec((nr, nc, 1, WIDTH), lambda bi, n: (bi, n, 0, 0)),
            row, row,
            pl.BlockSpec((LANES, LANES), lambda bi, n: (0, 0)),
        ],
        out_specs=pl.BlockSpec((nr, tb, WIDTH), lambda bi, n: (bi, n, 0)),
        out_shape=jax.ShapeDtypeStruct((b, t, WIDTH), BF16),
        scratch_shapes=[
            pltpu.VMEM((nr, N_PAIRS, LANES, LANES), F32),
            pltpu.VMEM((nr, tb, WIDTH), F32),
        ],
        compiler_params=_params("arbitrary", "arbitrary"),
        name="rwkv7",
    )(ops, eend, lnx_g, lnx_b, bd)


def _merge_kernel(oa_ref, ob_ref, oc_ref, g0_ref, g1_ref, g2_ref, x_ref, wa_ref, wb_ref, wc_ref, wo_ref, o_ref):
    tm = x_ref.shape[0]
    rc = MERGE_CHUNK

    def mixed(rows):
        def branch(o, w, g):
            return jax.nn.sigmoid(g[rows].astype(F32)) * _dot(o[rows], w[...])
        return (branch(oa_ref, wa_ref, g0_ref) + branch(ob_ref, wb_ref, g1_ref)
                + branch(oc_ref, wc_ref, g2_ref)).astype(BF16)

    chunks = [slice(r0, r0 + rc) for r0 in range(0, tm, rc)]
    pending = mixed(chunks[0])
    for i, rows in enumerate(chunks):
        nxt = mixed(chunks[i + 1]) if i + 1 < len(chunks) else None
        o_ref[rows] = x_ref[rows] + _dot(pending, wo_ref[...])
        pending = nxt


def _merge(x, proj, oa, ob, oc, wa, wb, wc, wo, layer, tm):
    n, d = x.shape
    branch = pl.BlockSpec((tm, WIDTH), lambda i: (i, 0))
    gate = lambda j: pl.BlockSpec((tm, d), lambda i: (i, COL_GATES // d + j))
    return pl.pallas_call(
        _merge_kernel,
        grid=(n // tm,),
        in_specs=[branch, branch, branch, gate(0), gate(1), gate(2),
                  pl.BlockSpec((tm, d), lambda i: (i, 0)),
                  _resident((WIDTH, d), layer), _resident((WIDTH, d), layer), _resident((WIDTH, d), layer),
                  _resident((d, d), layer)],
        out_specs=pl.BlockSpec((tm, d), lambda i: (i, 0)),
        out_shape=jax.ShapeDtypeStruct((n, d), F32),
        compiler_params=_params("arbitrary"),
        name="merge_out_proj",
    )(oa, ob, oc, proj, proj, proj, x, wa, wb, wc, wo)


def _ffn_kernel(x_ref, g_ref, wg_ref, wu_ref, wd_ref, o_ref):
    x = x_ref[...]
    h = (x * g_ref[...]).astype(BF16)
    rs = lax.rsqrt(jnp.mean(x * x, axis=-1, keepdims=True) + NORM_EPS)
    gate = _dot(h, wg_ref[...]) * rs
    up = _dot(h, wu_ref[...]) * rs
    act = (gate * jax.nn.sigmoid(gate) * up).astype(BF16)
    o_ref[...] = x + _dot(act, wd_ref[...])


def _ffn(x, g, wg, wu, wd, layer, tm):
    n, d = x.shape
    f = wg.shape[2]
    return pl.pallas_call(
        _ffn_kernel,
        grid=(n // tm,),
        in_specs=[pl.BlockSpec((tm, d), lambda i: (i, 0)), _resident((1, d)),
                  _resident((d, f), layer), _resident((d, f), layer), _resident((f, d), layer)],
        out_specs=pl.BlockSpec((tm, d), lambda i: (i, 0)),
        out_shape=jax.ShapeDtypeStruct((n, d), F32),
        compiler_params=_params("arbitrary"),
        name="swiglu_ffn",
    )(x, g, wg, wu, wd)


def _lora_weight(w2, a2, g2):
    w = jnp.zeros((2 * LANES, 3 * WIDTH), F32)
    w = w.at[0:64, 0:WIDTH].set(w2)
    w = w.at[64:128, WIDTH:2 * WIDTH].set(a2)
    w = w.at[128:256, 2 * WIDTH:3 * WIDTH].set(g2)
    return w.astype(BF16)


def _layer(x, consts, layer, big, norm1_g, attn_q_norm_g, attn_k_norm_g, attn_sinks,
           rwkv_shift_mu, rwkv_w0, rwkv_w2, rwkv_a0, rwkv_a2, rwkv_g2, rwkv_k_k, rwkv_k_a,
           rwkv_r_k, rwkv_lnx_g, rwkv_lnx_b, norm2_g):
    b, t, d = x.shape
    n = b * t
    bd, tri, ret_tables = consts
    w_in, w_attn_o, w_rwkv_o, w_ret_o, w_out, w_ffn_gate, w_ffn_up, w_ffn_down = big
    x2 = x.reshape(n, d)
    row = lambda a: a.reshape(1, WIDTH)
    prm = (rwkv_shift_mu.reshape(1, -1), row(rwkv_w0), row(rwkv_a0), row(rwkv_k_k), row(rwkv_k_a), row(rwkv_r_k),
           _lora_weight(rwkv_w2, rwkv_a2, rwkv_g2))
    tm = IN_PROJ_ROWS
    proj, ops, eend = _in_proj(x2, norm1_g.reshape(1, d), w_in, layer, prm, tri, bd, tm, t // tm)
    proj3 = proj.reshape(b, t, MAIN_WIDTH)

    o_a = _attention(proj3, attn_sinks, jnp.tile(attn_q_norm_g, N_HEADS).reshape(1, WIDTH),
                     jnp.tile(attn_k_norm_g, 2).reshape(1, LANES), bd, tq=ATTN_ROWS)

    o_b = _rwkv(ops.reshape(b, t, OPS_WIDTH), eend.reshape(b, t // RWKV_CHUNK, 1, WIDTH),
                row(rwkv_lnx_g), row(rwkv_lnx_b), bd)

    o_c = _retention(proj3, ret_tables, bd)

    x2 = _merge(x2, proj, o_a.reshape(n, WIDTH), o_b.reshape(n, WIDTH), o_c.reshape(n, WIDTH),
                w_attn_o, w_rwkv_o, w_ret_o, w_out, layer, tm=MERGE_ROWS)
    x2 = _ffn(x2, norm2_g.reshape(1, d), w_ffn_gate, w_ffn_up, w_ffn_down, layer, tm=FFN_ROWS)
    return x2.reshape(b, t, d)


def _constants(t):
    head_id = jnp.arange(LANES) // HEAD
    bd = (head_id[:, None] == head_id[None, :]).astype(BF16)
    idx = jnp.arange(RWKV_CHUNK)
    tri = (idx[None, :] <= idx[:, None]).astype(BF16)
    return bd, tri, _retention_tables(t)


def kernel(x, norm1_g, w_in, attn_q_norm_g, attn_k_norm_g, attn_sinks, w_attn_o, rwkv_shift_mu, rwkv_w0, rwkv_w2, rwkv_a0, rwkv_a2, rwkv_g2, rwkv_k_k, rwkv_k_a, rwkv_r_k, rwkv_lnx_g, rwkv_lnx_b, w_rwkv_o, w_ret_o, w_out, norm2_g, w_ffn_gate, w_ffn_up, w_ffn_down):
    consts = _constants(x.shape[1])
    big = tuple(w.astype(BF16) for w in (w_in, w_attn_o, w_rwkv_o, w_ret_o, w_out, w_ffn_gate, w_ffn_up, w_ffn_down))
    small = (norm1_g, attn_q_norm_g, attn_k_norm_g, attn_sinks, rwkv_shift_mu, rwkv_w0, rwkv_w2, rwkv_a0, rwkv_a2,
             rwkv_g2, rwkv_k_k, rwkv_k_a, rwkv_r_k, rwkv_lnx_g, rwkv_lnx_b, norm2_g)
    for layer in range(norm1_g.shape[0]):
        x = _layer(x, consts, layer, big, *(w[layer] for w in small))
    return x
```
